```python
import math
import jax
import jax.numpy as jnp
from jax import lax
import numpy as np

D_MODEL = 1024
BATCH = 2
SEQ = 16384
DEPTH = 2

MLA_HEADS = 8
MLA_NOPE = 64
MLA_ROPE = 32
MLA_V = 64
Q_LORA = 384
KV_LORA = 256
ROPE_THETA = 10000.0
SB_HEADS = 8
SB_HEAD_DIM = 64
MOBA_HEADS = 16
MOBA_HEAD_DIM = 64
MOBA_BLOCK = 256
MOBA_TOPK = 3
MOBA_GROUP = 128
REL_BUCKETS = 32
REL_MAX_EXACT = REL_BUCKETS // 2
REL_MAX_DIST = 128
D_FF_DENSE = 2048
N_EXPERTS = 8
TOP_K = 2
D_FF_EXPERT = 512
Q_BLOCK = 128
RMS_EPS = 1e-6
HI = lax.Precision.HIGHEST

EVEN_IN = Q_LORA + KV_LORA + MLA_ROPE + 3 * SB_HEADS * SB_HEAD_DIM
MIX_EVEN = MLA_HEADS * MLA_V + SB_HEADS * SB_HEAD_DIM
MIX_ODD = MOBA_HEADS * MOBA_HEAD_DIM
N_EVEN = (DEPTH + 1) // 2
N_ODD = DEPTH // 2

kernel_name = 'hybrid_mla_stickbreak_moba_moe'


def rmsnorm(x, g):
    xf = x.astype(jnp.float32)
    y = xf * lax.rsqrt(jnp.mean(xf * xf, axis=-1, keepdims=True) + RMS_EPS)
    return (y * g.astype(jnp.float32)).astype(x.dtype)


def to_heads(t, n_heads):
    b, s, _ = t.shape
    return t.reshape(b, s, n_heads, -1).transpose(0, 2, 1, 3)


def merge_heads(t):
    b, h, s, d = t.shape
    return t.transpose(0, 2, 1, 3).reshape(b, s, h * d)


def rope_tables(seq_len):
    inv = 1.0 / (ROPE_THETA ** (jnp.arange(0, MLA_ROPE, 2, dtype=jnp.float32) / MLA_ROPE))
    ang = jnp.arange(seq_len, dtype=jnp.float32)[:, None] * inv[None, :]
    return jnp.cos(ang), jnp.sin(ang)


def apply_rope(t, cos, sin):
    t1, t2 = jnp.split(t.astype(jnp.float32), 2, axis=-1)
    c = cos[None, :, None, :]
    s = sin[None, :, None, :]
    return jnp.concatenate([t1 * c - t2 * s, t2 * c + t1 * s], axis=-1).astype(t.dtype)


def causal_block_sweep(block_fn, q, k, v):
    outs = []
    for i in range(q.shape[2] // Q_BLOCK):
        end = (i + 1) * Q_BLOCK
        outs.append(block_fn(q[:, :, i * Q_BLOCK:end], k[:, :, :end], v[:, :, :end], i))
    return jnp.concatenate(outs, axis=2)


def mla_block(qi, ki, vi, i):
    scale = (MLA_NOPE + MLA_ROPE) ** -0.5
    qpos = i * Q_BLOCK + jnp.arange(Q_BLOCK)
    kpos = jnp.arange(ki.shape[2])
    logits = jnp.einsum('bhqd,bhkd->bhqk', qi, ki, preferred_element_type=jnp.float32) * scale
    logits = jnp.where(kpos[None, :] <= qpos[:, None], logits, -jnp.inf)
    p = jax.nn.softmax(logits, axis=-1)
    return jnp.einsum('bhqk,bhkd->bhqd', p.astype(vi.dtype), vi)


def stick_breaking_block(qi, ki, vi, i):
    b, h, _, _ = qi.shape
    n_kb = i + 1
    scale = SB_HEAD_DIM ** -0.5
    qpos = i * Q_BLOCK + jnp.arange(Q_BLOCK)
    kpos = jnp.arange(ki.shape[2])
    z = jnp.einsum('bhqd,bhkd->bhqk', qi, ki, preferred_element_type=jnp.float32) * scale
    past = kpos[None, :] < qpos[:, None]
    log_fail = jnp.where(past, jax.nn.log_sigmoid(-z), 0.0)
    lf = log_fail.reshape(b, h, Q_BLOCK, n_kb, Q_BLOCK)
    tri_in = jnp.tri(Q_BLOCK, k=-1, dtype=jnp.float32)
    tri_out = jnp.tri(n_kb, k=-1, dtype=jnp.float32)
    within = jnp.einsum('bhqnc,ce->bhqne', lf, tri_in, precision=HI)
    later_blocks = jnp.einsum('bhqm,mn->bhqn', jnp.sum(lf, axis=-1), tri_out, precision=HI)
    after = (within + later_blocks[..., None]).reshape(z.shape)
    w = jnp.where(past, jnp.exp(jax.nn.log_sigmoid(z) + after), 0.0)
    return jnp.einsum('bhqk,bhkd->bhqd', w.astype(vi.dtype), vi)


def t5_bucket(dist):
    n = jnp.maximum(dist, 0)
    nf = jnp.maximum(n, 1).astype(jnp.float32)
    large = REL_MAX_EXACT + (jnp.log(nf / REL_MAX_EXACT) / math.log(REL_MAX_DIST / REL_MAX_EXACT)
                             * (REL_BUCKETS - REL_MAX_EXACT)).astype(jnp.int32)
    large = jnp.minimum(large, REL_BUCKETS - 1)
    return jnp.where(n < REL_MAX_EXACT, n, large)


def moba_attention(q, k, v, rel_bias):
    b, h, s, d = q.shape
    scale = d ** -0.5
    nb = -(-s // MOBA_BLOCK)
    sp = nb * MOBA_BLOCK
    padw = ((0, 0), (0, 0), (0, sp - s), (0, 0))
    qb = jnp.pad(q, padw).reshape(b, h, nb, MOBA_BLOCK, d)
    kb = jnp.pad(k, padw).reshape(b, h, nb, MOBA_BLOCK, d)
    vb = jnp.pad(v, padw).reshape(b, h, nb, MOBA_BLOCK, d)
    offs = jnp.arange(MOBA_BLOCK)

    d_own = offs[:, None] - offs[None, :]
    l_own = jnp.einsum('bhnqd,bhnkd->bhnqk', qb, kb, preferred_element_type=jnp.float32) * scale
    l_own = l_own + rel_bias[:, t5_bucket(d_own)][None, :, None]
    l_own = jnp.where(d_own >= 0, l_own, -jnp.inf)
    m_own = jnp.max(l_own, axis=-1)
    p_own = jnp.exp(l_own - m_own[..., None])
    s_own = jnp.sum(p_own, axis=-1).reshape(b, h, sp)[:, :, :s]
    o_own = jnp.einsum('bhnqk,bhnkd->bhnqd', p_own.astype(v.dtype), vb,
                       preferred_element_type=jnp.float32).reshape(b, h, sp, d)[:, :, :s]
    m_own = m_own.reshape(b, h, sp)[:, :, :s]

    qpos = jnp.arange(s)
    qblk = qpos // MOBA_BLOCK
    k_mean = jnp.mean(kb.astype(jnp.float32), axis=3)
    gate = jnp.einsum('bhsd,bhnd->bhsn', q.astype(jnp.float32), k_mean)
    gate = jnp.where(jnp.arange(nb)[None, :] < qblk[:, None], gate, -jnp.inf)
    topk = min(MOBA_TOPK, nb)
    _, idx = lax.top_k(gate, topk)
    valid = idx < qblk[:, None]

    n_pairs = b * h * s * topk
    n_groups = b * h * nb
    bh = (jnp.arange(b)[:, None] * h + jnp.arange(h)[None, :])[:, :, None, None]
    group = (bh * nb + idx).reshape(-1)
    qrow = jnp.broadcast_to(bh * s + qpos[:, None], (b, h, s, topk)).reshape(-1)
    ppos = jnp.broadcast_to(qpos[:, None], (b, h, s, topk)).reshape(-1)
    pvalid = valid.reshape(-1)
    counts = jax.ops.segment_sum(jnp.ones((n_pairs,), jnp.int32), group, num_segments=n_groups)
    pcounts = (counts + MOBA_GROUP - 1) // MOBA_GROUP * MOBA_GROUP
    pend = jnp.cumsum(pcounts)
    pstart = pend - pcounts
    start = jnp.cumsum(counts) - counts
    order = jnp.argsort(group).astype(jnp.int32)
    g_sorted = group[order]
    dst_sorted = pstart[g_sorted] + jnp.arange(n_pairs, dtype=jnp.int32) - start[g_sorted]
    n_units = -(-n_pairs // MOBA_GROUP) + n_groups
    cap = n_units * MOBA_GROUP
    slot_pair = jnp.full((cap,), n_pairs, jnp.int32).at[dst_sorted].set(order).reshape(n_units, MOBA_GROUP)
    dst = jnp.zeros((n_pairs,), jnp.int32).at[order].set(dst_sorted)
    unit_group = jnp.minimum(jnp.searchsorted(pend, jnp.arange(n_units) * MOBA_GROUP, side='right'),
                             n_groups - 1)
    qrow_u = jnp.append(qrow, 0)[slot_pair]
    pos_u = jnp.append(ppos, 0)[slot_pair]
    ok_u = jnp.append(pvalid, False)[slot_pair]

    q_u = q.reshape(b * h * s, d)[qrow_u]
    k_u = kb.reshape(n_groups, MOBA_BLOCK, d)[unit_group]
    v_u = vb.reshape(n_groups, MOBA_BLOCK, d)[unit_group]
    head_u = (unit_group // nb) % h
    blk_u = unit_group % nb
    dist = pos_u[:, :, None] - (blk_u[:, None, None] * MOBA_BLOCK + offs)
    l_u = jnp.einsum('ugd,ucd->ugc', q_u, k_u, preferred_element_type=jnp.float32) * scale
    l_u = l_u + rel_bias[head_u[:, None, None], t5_bucket(dist)]
    l_u = jnp.where(ok_u[..., None], l_u, -jnp.inf)
    m_u = jnp.where(ok_u, jnp.max(l_u, axis=-1), 0.0)
    p_u = jnp.exp(l_u - m_u[..., None])
    s_u = jnp.sum(p_u, axis=-1)
    o_u = jnp.einsum('ugc,ucd->ugd', p_u.astype(v.dtype), v_u, preferred_element_type=jnp.float32)

    m_sel = jnp.where(valid, m_u.reshape(cap)[dst].reshape(b, h, s, topk), -jnp.inf)
    s_sel = s_u.reshape(cap)[dst].reshape(b, h, s, topk)
    o_sel = o_u.reshape(cap, d)[dst].reshape(b, h, s, topk, d)
    m_all = jnp.maximum(m_own, jnp.max(m_sel, axis=-1))
    w_own = jnp.exp(m_own - m_all)
    w_sel = jnp.exp(m_sel - m_all[..., None])
    num = w_own[..., None] * o_own + jnp.einsum('bhsk,bhskd->bhsd', w_sel, o_sel)
    den = w_own * s_own + jnp.sum(w_sel * s_sel, axis=-1)
    return (num / den[..., None]).astype(q.dtype)


def swiglu(h, w_gate, w_up, w_down):
    return (jax.nn.silu(h @ w_gate) * (h @ w_up)) @ w_down


def moe_swiglu(h, w_router, w_gate, w_up, w_down):
    logits = (h @ w_router).astype(jnp.float32)
    top_val, top_idx = lax.top_k(logits, TOP_K)
    top_w = jax.nn.softmax(top_val, axis=-1)
    gates = jnp.sum(jax.nn.one_hot(top_idx, N_EXPERTS, dtype=jnp.float32) * top_w[..., None], axis=-2)
    out = jnp.zeros_like(h)
    for e in range(N_EXPERTS):
        y = swiglu(h, w_gate[e], w_up[e], w_down[e])
        out = out + (gates[..., e:e + 1] * y).astype(h.dtype)
    return out


def mla_sb_mixer(h, w_in, q_norm, w_uq, kv_norm, w_ukv, w_o, cos, sin):
    b, s, _ = h.shape
    proj = h @ w_in
    c_q, c_kv, k_rope, sb_qkv = jnp.split(
        proj, [Q_LORA, Q_LORA + KV_LORA, Q_LORA + KV_LORA + MLA_ROPE], axis=-1)
    q = (rmsnorm(c_q, q_norm) @ w_uq).reshape(b, s, MLA_HEADS, MLA_NOPE + MLA_ROPE)
    q_nope, q_rope = jnp.split(q, [MLA_NOPE], axis=-1)
    q_rope = apply_rope(q_rope, cos, sin)
    kv = (rmsnorm(c_kv, kv_norm) @ w_ukv).reshape(b, s, MLA_HEADS, MLA_NOPE + MLA_V)
    k_nope, v_a = jnp.split(kv, [MLA_NOPE], axis=-1)
    k_rope = apply_rope(k_rope[:, :, None, :], cos, sin)
    k_a = jnp.concatenate([k_nope, jnp.broadcast_to(k_rope, (b, s, MLA_HEADS, MLA_ROPE))], axis=-1)
    q_a = jnp.concatenate([q_nope, q_rope], axis=-1)
    o_a = causal_block_sweep(mla_block, q_a.transpose(0, 2, 1, 3), k_a.transpose(0, 2, 1, 3),
                             v_a.transpose(0, 2, 1, 3))
    q_b, k_b, v_b = jnp.split(sb_qkv, 3, axis=-1)
    o_b = causal_block_sweep(stick_breaking_block, to_heads(q_b, SB_HEADS), to_heads(k_b, SB_HEADS),
                             to_heads(v_b, SB_HEADS))
    return jnp.concatenate([merge_heads(o_a), merge_heads(o_b)], axis=-1) @ w_o


def moba_mixer(h, w_qkv, w_o, rel_bias):
    q, k, v = jnp.split(h @ w_qkv, 3, axis=-1)
    o = moba_attention(to_heads(q, MOBA_HEADS), to_heads(k, MOBA_HEADS), to_heads(v, MOBA_HEADS), rel_bias)
    return merge_heads(o) @ w_o


def setup_inputs(seed: int = 0) -> dict:
    key = jax.random.key(seed)
    ks = jax.random.split(key, 22)

    def dense(k, shape):
        return jax.random.normal(k, shape, jnp.float32) * (shape[-2] ** -0.5)

    def gain(k, shape):
        return 1.0 + 0.02 * jax.random.normal(k, shape, jnp.float32)

    return {
        'x': jax.random.normal(ks[0], (BATCH, SEQ, D_MODEL), jnp.float32),
        'ev_attn_norm': gain(ks[1], (N_EVEN, D_MODEL)),
        'ev_w_in': dense(ks[2], (N_EVEN, D_MODEL, EVEN_IN)),
        'ev_q_norm': gain(ks[3], (N_EVEN, Q_LORA)),
        'ev_w_uq': dense(ks[4], (N_EVEN, Q_LORA, MLA_HEADS * (MLA_NOPE + MLA_ROPE))),
        'ev_kv_norm': gain(ks[5], (N_EVEN, KV_LORA)),
        'ev_w_ukv': dense(ks[6], (N_EVEN, KV_LORA, MLA_HEADS * (MLA_NOPE + MLA_V))),
        'ev_w_o': dense(ks[7], (N_EVEN, MIX_EVEN, D_MODEL)),
        'ev_ffn_norm': gain(ks[8], (N_EVEN, D_MODEL)),
        'ev_w_gate': dense(ks[9], (N_EVEN, D_MODEL, D_FF_DENSE)),
        'ev_w_up': dense(ks[10], (N_EVEN, D_MODEL, D_FF_DENSE)),
        'ev_w_down': dense(ks[11], (N_EVEN, D_FF_DENSE, D_MODEL)),
        'od_attn_norm': gain(ks[12], (N_ODD, D_MODEL)),
        'od_w_qkv': dense(ks[13], (N_ODD, D_MODEL, 3 * MIX_ODD)),
        'od_w_o': dense(ks[14], (N_ODD, MIX_ODD, D_MODEL)),
        'od_ffn_norm': gain(ks[15], (N_ODD, D_MODEL)),
        'od_w_router': dense(ks[16], (N_ODD, D_MODEL, N_EXPERTS)),
        'od_w_gate': dense(ks[17], (N_ODD, N_EXPERTS, D_MODEL, D_FF_EXPERT)),
        'od_w_up': dense(ks[18], (N_ODD, N_EXPERTS, D_MODEL, D_FF_EXPERT)),
        'od_w_down': dense(ks[19], (N_ODD, N_EXPERTS, D_FF_EXPERT, D_MODEL)),
        'rel_bias': 0.5 * jax.random.normal(ks[20], (MOBA_HEADS, REL_BUCKETS), jnp.float32),
        'final_norm': gain(ks[21], (D_MODEL,)),
    }


def reference(x, ev_attn_norm, ev_w_in, ev_q_norm, ev_w_uq, ev_kv_norm, ev_w_ukv, ev_w_o,
              ev_ffn_norm, ev_w_gate, ev_w_up, ev_w_down,
              od_attn_norm, od_w_qkv, od_w_o, od_ffn_norm, od_w_router, od_w_gate, od_w_up, od_w_down,
              rel_bias, final_norm):
    cos, sin = rope_tables(x.shape[1])
    for layer in range(DEPTH):
        j = layer // 2
        if layer % 2 == 0:
            x = x + mla_sb_mixer(rmsnorm(x, ev_attn_norm[j]), ev_w_in[j], ev_q_norm[j], ev_w_uq[j],
                                 ev_kv_norm[j], ev_w_ukv[j], ev_w_o[j], cos, sin)
            x = x + swiglu(rmsnorm(x, ev_ffn_norm[j]), ev_w_gate[j], ev_w_up[j], ev_w_down[j])
        else:
            x = x + moba_mixer(rmsnorm(x, od_attn_norm[j]), od_w_qkv[j], od_w_o[j], rel_bias)
            x = x + moe_swiglu(rmsnorm(x, od_ffn_norm[j]), od_w_router[j], od_w_gate[j],
                               od_w_up[j], od_w_down[j])
    return rmsnorm(x, final_norm)
```

```python
import functools
import math

import jax
import jax.numpy as jnp
import numpy as np
from jax import lax
from jax.experimental import pallas as pl
from jax.experimental.pallas import tpu as pltpu

MLA_HEADS = 8
MLA_NOPE = 64
MLA_ROPE = 32
MLA_V = 64
Q_LORA = 384
KV_LORA = 256
ROPE_THETA = 10000.0
SB_HEADS = 8
SB_HEAD_DIM = 64
MOBA_HEADS = 16
MOBA_HEAD_DIM = 64
MOBA_BLOCK = 256
MOBA_TOPK = 3
REL_BUCKETS = 32
REL_MAX_EXACT = REL_BUCKETS // 2
REL_MAX_DIST = 128
N_EXPERTS = 8
TOP_K = 2
RMS_EPS = 1e-6

LANES = 128
VMEM_LIMIT = 56 * 1024 * 1024
ROW_TILE = 512
MOE_ROW_TILE = 1024
MLA_TILE = 512
SB_TILE = 256
SB_UNDERFLOW = -128.0
LOG2E = 1.4426950408889634

BF16 = jnp.bfloat16
F32 = jnp.float32


def _params(*sem):
    return pltpu.CompilerParams(dimension_semantics=sem, vmem_limit_bytes=VMEM_LIMIT)


def _rms(x, g):
    return x * lax.rsqrt(jnp.mean(x * x, axis=-1, keepdims=True) + RMS_EPS) * g


def _dot(a, b):
    return jnp.dot(a, b, preferred_element_type=F32)


def _dot_nt(a, b):
    return lax.dot_general(a, b, (((1,), (1,)), ((), ())), preferred_element_type=F32)


def _full(shape):
    return pl.BlockSpec(shape, lambda *_: (0,) * len(shape))


def _l0_proj_kernel(x_ref, g_ref, wcq_ref, wckv_ref, wkr_ref, wsb_ref, qn_ref, wuqn_ref, wuqr_ref,
                    kvn_ref, wukk_ref, wukv_ref, cos_ref, sin_ref,
                    qnope_ref, qrope_ref, knope_ref, krope_ref, va_ref, sbq_ref, sbk_ref, sbv_ref):
    hn = _rms(x_ref[...], g_ref[...]).astype(BF16)
    c = cos_ref[...]
    s = sin_ref[...]

    def rope(t):
        t1 = t[:, :LANES]
        t2 = t[:, LANES:]
        return t1 * c - t2 * s, t2 * c + t1 * s

    cq = _rms(_dot(hn, wcq_ref[...]), qn_ref[...]).astype(BF16)
    qnope_ref[...] = _dot(cq, wuqn_ref[...]).astype(BF16)
    r1, r2 = rope(_dot(cq, wuqr_ref[...]))
    qrope_ref[:, :LANES] = r1.astype(BF16)
    qrope_ref[:, LANES:] = r2.astype(BF16)

    ckv = _rms(_dot(hn, wckv_ref[...]), kvn_ref[...]).astype(BF16)
    knope_ref[...] = _dot(ckv, wukk_ref[...]).astype(BF16)
    va_ref[...] = _dot(ckv, wukv_ref[...]).astype(BF16)

    k1, k2 = rope(_dot(hn, wkr_ref[...]))
    krope_ref[:, :LANES] = k1.astype(BF16)
    krope_ref[:, LANES:] = k2.astype(BF16)

    d = SB_HEADS * SB_HEAD_DIM
    sb = _dot(hn, wsb_ref[...])
    sbq_ref[...] = (sb[:, :d] * (SB_HEAD_DIM ** -0.5)).astype(BF16)
    sbk_ref[...] = sb[:, d:2 * d].astype(BF16)
    sbv_ref[...] = sb[:, 2 * d:].astype(BF16)


def _l0_proj(x2, g, w_in, q_norm, w_uq, kv_norm, w_ukv, cos_t, sin_t, seq):
    rows, dm = x2.shape
    tm = min(ROW_TILE, seq)
    half = MLA_ROPE // 2
    o_kr = Q_LORA + KV_LORA
    o_sb = o_kr + MLA_ROPE
    zpad = jnp.zeros((dm, LANES - half), F32)
    w_cq = w_in[:, :Q_LORA].astype(BF16)
    w_ckv = w_in[:, Q_LORA:o_kr].astype(BF16)
    w_kr = jnp.concatenate([w_in[:, o_kr:o_kr + half], zpad, w_in[:, o_kr + half:o_sb], zpad],
                           axis=1).astype(BF16)
    w_sb = w_in[:, o_sb:].astype(BF16)
    uq = w_uq.reshape(Q_LORA, MLA_HEADS, MLA_NOPE + MLA_ROPE)
    w_uqn = uq[:, :, :MLA_NOPE].reshape(Q_LORA, -1).astype(BF16)
    w_uqr = jnp.concatenate([uq[:, :, MLA_NOPE:MLA_NOPE + half].reshape(Q_LORA, -1),
                             uq[:, :, MLA_NOPE + half:].reshape(Q_LORA, -1)], axis=1).astype(BF16)
    ukv = w_ukv.reshape(KV_LORA, MLA_HEADS, MLA_NOPE + MLA_V)
    w_ukk = ukv[:, :, :MLA_NOPE].reshape(KV_LORA, -1).astype(BF16)
    w_ukv2 = ukv[:, :, MLA_NOPE:].reshape(KV_LORA, -1).astype(BF16)
    n_s = seq // tm
    row = lambda n: pl.BlockSpec((tm, n), lambda i: (i, 0))
    tab = pl.BlockSpec((tm, LANES), lambda i: (i % n_s, 0))
    d_a = MLA_HEADS * MLA_NOPE
    d_b = SB_HEADS * SB_HEAD_DIM
    out_widths = [d_a, 2 * LANES, d_a, 2 * LANES, MLA_HEADS * MLA_V, d_b, d_b, d_b]
    return pl.pallas_call(
        _l0_proj_kernel,
        grid=(rows // tm,),
        in_specs=[row(dm), _full((1, dm)), _full(w_cq.shape), _full(w_ckv.shape), _full(w_kr.shape),
                  _full(w_sb.shape), _full((1, Q_LORA)), _full(w_uqn.shape), _full(w_uqr.shape),
                  _full((1, KV_LORA)), _full(w_ukk.shape), _full(w_ukv2.shape), tab, tab],
        out_specs=[row(n) for n in out_widths],
        out_shape=[jax.ShapeDtypeStruct((rows, n), BF16) for n in out_widths],
        compiler_params=_params("parallel"),
    )(x2, g.reshape(1, dm), w_cq, w_ckv, w_kr, w_sb, q_norm.reshape(1, -1), w_uqn, w_uqr,
      kv_norm.reshape(1, -1), w_ukk, w_ukv2, cos_t, sin_t)


def _mla_kernel(q_ref, k_ref, v_ref, o_ref, m_sc, l_sc, acc_sc, *, t, c):
    qi = pl.program_id(2)
    q = q_ref[...]
    m_sc[...] = jnp.full_like(m_sc, -jnp.inf)
    l_sc[...] = jnp.zeros_like(l_sc)
    acc_sc[...] = jnp.zeros_like(acc_sc)

    def tile(j, diagonal):
        off = pl.multiple_of(j * t, t)
        s = _dot_nt(q, k_ref[pl.ds(off, t), :])
        if diagonal:
            r = lax.broadcasted_iota(jnp.int32, (t, t), 0)
            cc = lax.broadcasted_iota(jnp.int32, (t, t), 1)
            s = jnp.where(cc <= r, s, -jnp.inf)
        m_prev = m_sc[...]
        m_new = jnp.maximum(m_prev, jnp.max(s, axis=-1, keepdims=True))
        alpha = jnp.exp2((m_prev - m_new) * c)
        p = jnp.exp2((s - m_new) * c)
        l_sc[...] = alpha * l_sc[...] + jnp.sum(p, axis=-1, keepdims=True)
        acc_sc[...] = alpha * acc_sc[...] + _dot(p.astype(BF16), v_ref[pl.ds(off, t), :])
        m_sc[...] = m_new

    def body(j, carry):
        tile(j, False)
        return carry

    lax.fori_loop(0, qi, body, 0)
    tile(qi, True)
    o_ref[...] = (acc_sc[...] / l_sc[...]).astype(o_ref.dtype)


def _mla_attention(q, k, v):
    b, h, s, dq = q.shape
    dv = v.shape[-1]
    t = min(MLA_TILE, s)
    c = (MLA_NOPE + MLA_ROPE) ** -0.5 * LOG2E
    return pl.pallas_call(
        functools.partial(_mla_kernel, t=t, c=c),
        grid=(b, h, s // t),
        in_specs=[pl.BlockSpec((None, None, t, dq), lambda bi, hi, i: (bi, hi, i, 0)),
                  pl.BlockSpec((None, None, s, dq), lambda bi, hi, i: (bi, hi, 0, 0)),
                  pl.BlockSpec((None, None, s, dv), lambda bi, hi, i: (bi, hi, 0, 0))],
        out_specs=pl.BlockSpec((None, None, t, dv), lambda bi, hi, i: (bi, hi, i, 0)),
        out_shape=jax.ShapeDtypeStruct((b, h, s, dv), BF16),
        scratch_shapes=[pltpu.VMEM((t, 1), F32), pltpu.VMEM((t, 1), F32), pltpu.VMEM((t, dv), F32)],
        compiler_params=_params("parallel", "parallel", "arbitrary"),
    )(q, k, v)


def _split3(x):
    hi = x.astype(BF16)
    r = x - hi.astype(F32)
    mid = r.astype(BF16)
    lo = (r - mid.astype(F32)).astype(BF16)
    return hi, mid, lo


def _sb_kernel(q_ref, k_ref, v_ref, o_ref, acc_sc, run_sc, *, t):
    qi = pl.program_id(2)
    q = q_ref[...]
    acc_sc[...] = jnp.zeros_like(acc_sc)
    run_sc[...] = jnp.zeros_like(run_sc)
    r = lax.broadcasted_iota(jnp.int32, (t, t), 0)
    cc = lax.broadcasted_iota(jnp.int32, (t, t), 1)
    later = (r > cc).astype(BF16)

    def tile(j, diagonal):
        off = pl.multiple_of(j * t, t)
        z = _dot_nt(q, k_ref[pl.ds(off, t), :])
        log_fail = jnp.minimum(-z, 0.0) - jnp.log(1.0 + jnp.exp(-jnp.abs(z)))
        log_hit = log_fail + z
        if diagonal:
            past = cc < r
            log_fail = jnp.where(past, log_fail, 0.0)
        hi, mid, lo = _split3(log_fail)
        within = _dot(hi, later) + _dot(mid, later) + _dot(lo, later)
        w = jnp.exp(log_hit + within + run_sc[...])
        if diagonal:
            w = jnp.where(past, w, 0.0)
        acc_sc[...] += _dot(w.astype(BF16), v_ref[pl.ds(off, t), :])
        run_sc[...] += jnp.sum(log_fail, axis=-1, keepdims=True)

    tile(qi, True)

    def cond(carry):
        j, worst = carry
        return jnp.logical_and(j >= 0, worst > SB_UNDERFLOW)

    def body(carry):
        j, _ = carry
        tile(j, False)
        return j - 1, jnp.max(run_sc[...])

    lax.while_loop(cond, body, (qi - 1, jnp.max(run_sc[...])))
    o_ref[...] = acc_sc[...].astype(o_ref.dtype)


def _sb_attention(q, k, v):
    b, h, s, d = q.shape
    t = min(SB_TILE, s)
    return pl.pallas_call(
        functools.partial(_sb_kernel, t=t),
        grid=(b, h, s // t),
        in_specs=[pl.BlockSpec((None, None, t, d), lambda bi, hi, i: (bi, hi, i, 0)),
                  pl.BlockSpec((None, None, s, d), lambda bi, hi, i: (bi, hi, 0, 0)),
                  pl.BlockSpec((None, None, s, d), lambda bi, hi, i: (bi, hi, 0, 0))],
        out_specs=pl.BlockSpec((None, None, t, d), lambda bi, hi, i: (bi, hi, i, 0)),
        out_shape=jax.ShapeDtypeStruct((b, h, s, d), BF16),
        scratch_shapes=[pltpu.VMEM((t, d), F32), pltpu.VMEM((t, 1), F32)],
        compiler_params=_params("parallel", "parallel", "arbitrary"),
    )(q, k, v)


def _moba_kernel(q_ref, k_ref, v_ref, bown_ref, bprev_ref, bfar_ref, o_ref,
                 kmean_sc, sel_sc, m_sc, l_sc, acc_sc, *, nb):
    blk = MOBA_BLOCK
    i = pl.program_id(2)

    @pl.when(i == 0)
    def _():
        kf = k_ref[...].astype(F32).reshape(nb, blk, MOBA_HEAD_DIM)
        kmean_sc[...] = jnp.sum(kf, axis=1) * (1.0 / blk)

    q = q_ref[...]
    k1, k2, k3 = _split3(kmean_sc[...])
    gate = _dot_nt(q, k1) + _dot_nt(q, k2) + _dot_nt(q, k3)
    lane = lax.broadcasted_iota(jnp.int32, (blk, nb), 1)
    gate = jnp.where(lane < i, gate, -jnp.inf)
    sel = jnp.zeros((blk, nb), F32)
    for _ in range(MOBA_TOPK):
        mx = jnp.max(gate, axis=-1, keepdims=True)
        is_max = jnp.logical_and(gate == mx, mx > -jnp.inf)
        idx = jnp.min(jnp.where(is_max, lane, nb), axis=-1, keepdims=True)
        pick = lane == idx
        sel = jnp.where(pick, 1.0, sel)
        gate = jnp.where(pick, -jnp.inf, gate)
    sel_sc[...] = sel

    off_own = pl.multiple_of(i * blk, blk)
    r = lax.broadcasted_iota(jnp.int32, (blk, blk), 0)
    cc = lax.broadcasted_iota(jnp.int32, (blk, blk), 1)
    s = _dot_nt(q, k_ref[pl.ds(off_own, blk), :]) + bown_ref[...]
    s = jnp.where(cc <= r, s, -jnp.inf)
    m0 = jnp.max(s, axis=-1, keepdims=True)
    p = jnp.exp(s - m0)
    m_sc[...] = m0
    l_sc[...] = jnp.sum(p, axis=-1, keepdims=True)
    acc_sc[...] = _dot(p.astype(BF16), v_ref[pl.ds(off_own, blk), :])

    def past_block(n, bias):
        off = pl.multiple_of(n * blk, blk)
        chosen = jnp.sum(jnp.where(lane == n, sel_sc[...], 0.0), axis=-1, keepdims=True) > 0.0
        s = _dot_nt(q, k_ref[pl.ds(off, blk), :]) + bias
        s = jnp.where(chosen, s, -jnp.inf)
        m_prev = m_sc[...]
        m_new = jnp.maximum(m_prev, jnp.max(s, axis=-1, keepdims=True))
        alpha = jnp.exp(m_prev - m_new)
        p = jnp.exp(s - m_new)
        l_sc[...] = alpha * l_sc[...] + jnp.sum(p, axis=-1, keepdims=True)
        acc_sc[...] = alpha * acc_sc[...] + _dot(p.astype(BF16), v_ref[pl.ds(off, blk), :])
        m_sc[...] = m_new

    def body(n, carry):
        past_block(n, bfar_ref[...])
        return carry

    lax.fori_loop(0, i - 1, body, 0)

    @pl.when(i >= 1)
    def _():
        past_block(i - 1, bprev_ref[...])

    o_ref[...] = (acc_sc[...] / l_sc[...]).astype(o_ref.dtype)


def _t5_bucket(dist):
    n = jnp.maximum(dist, 0)
    nf = jnp.maximum(n, 1).astype(F32)
    large = REL_MAX_EXACT + (jnp.log(nf / REL_MAX_EXACT) / math.log(REL_MAX_DIST / REL_MAX_EXACT)
                             * (REL_BUCKETS - REL_MAX_EXACT)).astype(jnp.int32)
    large = jnp.minimum(large, REL_BUCKETS - 1)
    return jnp.where(n < REL_MAX_EXACT, n, large)


def _moba_attention(q, k, v, rel_bias):
    b, h, s, d = q.shape
    blk = MOBA_BLOCK
    assert s % blk == 0 and d == MOBA_HEAD_DIM
    assert 2 * blk > REL_MAX_DIST
    nb = s // blk
    offs = jnp.arange(blk)
    d_own = offs[:, None] - offs[None, :]
    b_own = rel_bias[:, _t5_bucket(d_own)]
    b_prev = rel_bias[:, _t5_bucket(d_own + blk)]
    b_far = jnp.broadcast_to(rel_bias[:, REL_BUCKETS - 1][:, None, None], (h, 1, blk))
    return pl.pallas_call(
        functools.partial(_moba_kernel, nb=nb),
        grid=(b, h, nb),
        in_specs=[pl.BlockSpec((None, None, blk, d), lambda bi, hi, i: (bi, hi, i, 0)),
                  pl.BlockSpec((None, None, s, d), lambda bi, hi, i: (bi, hi, 0, 0)),
                  pl.BlockSpec((None, None, s, d), lambda bi, hi, i: (bi, hi, 0, 0)),
                  pl.BlockSpec((None, blk, blk), lambda bi, hi, i: (hi, 0, 0)),
                  pl.BlockSpec((None, blk, blk), lambda bi, hi, i: (hi, 0, 0)),
                  pl.BlockSpec((None, 1, blk), lambda bi, hi, i: (hi, 0, 0))],
        out_specs=pl.BlockSpec((None, None, blk, d), lambda bi, hi, i: (bi, hi, i, 0)),
        out_shape=jax.ShapeDtypeStruct((b, h, s, d), BF16),
        scratch_shapes=[pltpu.VMEM((nb, d), F32), pltpu.VMEM((blk, nb), F32), pltpu.VMEM((blk, 1), F32),
                        pltpu.VMEM((blk, 1), F32), pltpu.VMEM((blk, d), F32)],
        compiler_params=_params("parallel", "parallel", "arbitrary"),
    )(q, k, v, b_own, b_prev, b_far)


def _norm_proj_kernel(x_ref, g_ref, w_ref, q_ref, k_ref, v_ref, *, width):
    hn = _rms(x_ref[...], g_ref[...]).astype(BF16)
    y = _dot(hn, w_ref[...])
    q_ref[...] = (y[:, :width] * (MOBA_HEAD_DIM ** -0.5)).astype(BF16)
    k_ref[...] = y[:, width:2 * width].astype(BF16)
    v_ref[...] = y[:, 2 * width:].astype(BF16)


def _l1_proj(x2, g, w_qkv):
    rows, dm = x2.shape
    tm = min(ROW_TILE, rows)
    width = w_qkv.shape[1] // 3
    row = lambda n: pl.BlockSpec((tm, n), lambda i: (i, 0))
    return pl.pallas_call(
        functools.partial(_norm_proj_kernel, width=width),
        grid=(rows // tm,),
        in_specs=[row(dm), _full((1, dm)), _full(w_qkv.shape)],
        out_specs=[row(width)] * 3,
        out_shape=[jax.ShapeDtypeStruct((rows, width), BF16)] * 3,
        compiler_params=_params("parallel"),
    )(x2, g.reshape(1, dm), w_qkv.astype(BF16))


def _out_proj_kernel(a_ref, w_ref, x_ref, o_ref):
    o_ref[...] = x_ref[...] + _dot(a_ref[...], w_ref[...])


def _out_proj(a, w, x2):
    rows, dm = x2.shape
    tm = min(ROW_TILE, rows)
    return pl.pallas_call(
        _out_proj_kernel,
        grid=(rows // tm,),
        in_specs=[pl.BlockSpec((tm, a.shape[1]), lambda i: (i, 0)), _full(w.shape),
                  pl.BlockSpec((tm, dm), lambda i: (i, 0))],
        out_specs=pl.BlockSpec((tm, dm), lambda i: (i, 0)),
        out_shape=jax.ShapeDtypeStruct((rows, dm), F32),
        compiler_params=_params("parallel"),
    )(a, w.astype(BF16), x2)


def _silu(x):
    return x / (1.0 + jnp.exp(-x))


def _swiglu_kernel(x_ref, g_ref, wg_ref, wu_ref, wd_ref, o_ref, *, chunk):
    x = x_ref[...]
    hn = _rms(x, g_ref[...]).astype(BF16)
    o_ref[...] = x
    for c0 in range(0, wg_ref.shape[1], chunk):
        a = _silu(_dot(hn, wg_ref[:, c0:c0 + chunk])) * _dot(hn, wu_ref[:, c0:c0 + chunk])
        o_ref[...] += _dot(a.astype(BF16), wd_ref[c0:c0 + chunk, :])


def _swiglu(x2, g, w_gate, w_up, w_down):
    rows, dm = x2.shape
    tm = min(ROW_TILE, rows)
    return pl.pallas_call(
        functools.partial(_swiglu_kernel, chunk=512),
        grid=(rows // tm,),
        in_specs=[pl.BlockSpec((tm, dm), lambda i: (i, 0)), _full((1, dm)), _full(w_gate.shape),
                  _full(w_up.shape), _full(w_down.shape)],
        out_specs=pl.BlockSpec((tm, dm), lambda i: (i, 0)),
        out_shape=jax.ShapeDtypeStruct((rows, dm), F32),
        compiler_params=_params("parallel"),
    )(x2, g.reshape(1, dm), w_gate.astype(BF16), w_up.astype(BF16), w_down.astype(BF16))


def _moe_kernel(x_ref, g_ref, wr_ref, wg_ref, wu_ref, wd_ref, gf_ref, o_ref, hn_sc, gate_sc, acc_sc):
    e = pl.program_id(1)
    lane = lax.broadcasted_iota(jnp.int32, gate_sc.shape, 1)

    @pl.when(e == 0)
    def _():
        h = _rms(x_ref[...], g_ref[...])
        hn_sc[...] = h.astype(BF16)
        logits = jnp.dot(h, wr_ref[...], preferred_element_type=F32, precision=lax.Precision.HIGHEST)
        logits = jnp.where(lane < N_EXPERTS, logits, -jnp.inf)
        v1 = jnp.max(logits, axis=-1, keepdims=True)
        i1 = jnp.min(jnp.where(logits == v1, lane, LANES), axis=-1, keepdims=True)
        rest = jnp.where(lane == i1, -jnp.inf, logits)
        v2 = jnp.max(rest, axis=-1, keepdims=True)
        i2 = jnp.min(jnp.where(rest == v2, lane, LANES), axis=-1, keepdims=True)
        p2 = jnp.exp(v2 - v1)
        w1 = 1.0 / (1.0 + p2)
        gate_sc[...] = jnp.where(lane == i1, w1, 0.0) + jnp.where(lane == i2, p2 * w1, 0.0)
        acc_sc[...] = jnp.zeros_like(acc_sc)

    hn = hn_sc[...]
    gate = jnp.sum(jnp.where(lane == e, gate_sc[...], 0.0), axis=-1, keepdims=True)
    a = _silu(_dot(hn, wg_ref[...])) * _dot(hn, wu_ref[...]) * gate
    acc_sc[...] += _dot(a.astype(BF16), wd_ref[...])

    @pl.when(e == pl.num_programs(1) - 1)
    def _():
        o_ref[...] = _rms(x_ref[...] + acc_sc[...], gf_ref[...])


def _moe_final(x2, g, w_router, w_gate, w_up, w_down, g_final):
    rows, dm = x2.shape
    tm = min(MOE_ROW_TILE, rows)
    ne, _, dff = w_gate.shape
    w_r = jnp.zeros((dm, LANES), F32).at[:, :ne].set(w_router)
    return pl.pallas_call(
        _moe_kernel,
        grid=(rows // tm, ne),
        in_specs=[pl.BlockSpec((tm, dm), lambda i, e: (i, 0)),
                  pl.BlockSpec((1, dm), lambda i, e: (0, 0)),
                  pl.BlockSpec((dm, LANES), lambda i, e: (0, 0)),
                  pl.BlockSpec((None, dm, dff), lambda i, e: (e, 0, 0)),
                  pl.BlockSpec((None, dm, dff), lambda i, e: (e, 0, 0)),
                  pl.BlockSpec((None, dff, dm), lambda i, e: (e, 0, 0)),
                  pl.BlockSpec((1, dm), lambda i, e: (0, 0))],
        out_specs=pl.BlockSpec((tm, dm), lambda i, e: (i, 0)),
        out_shape=jax.ShapeDtypeStruct((rows, dm), F32),
        scratch_shapes=[pltpu.VMEM((tm, dm), BF16), pltpu.VMEM((tm, LANES), F32), pltpu.VMEM((tm, dm), F32)],
        compiler_params=_params("parallel", "arbitrary"),
    )(x2, g.reshape(1, dm), w_r, w_gate.astype(BF16), w_up.astype(BF16), w_down.astype(BF16),
      g_final.reshape(1, dm))


def _to_heads(t, b, s, n_heads):
    return t.reshape(b, s, n_heads, -1).transpose(0, 2, 1, 3)


def _merge_heads(t):
    b, h, s, d = t.shape
    return t.transpose(0, 2, 1, 3).reshape(b * s, h * d)


def _rope_tables(seq):
    half = MLA_ROPE // 2
    inv = 1.0 / (ROPE_THETA ** (jnp.arange(0, MLA_ROPE, 2, dtype=F32) / MLA_ROPE))
    ang = jnp.arange(seq, dtype=F32)[:, None] * inv[None, :]
    reps = LANES // half
    return jnp.tile(jnp.cos(ang), (1, reps)), jnp.tile(jnp.sin(ang), (1, reps))


def kernel(x, ev_attn_norm, ev_w_in, ev_q_norm, ev_w_uq, ev_kv_norm, ev_w_ukv, ev_w_o, ev_ffn_norm, ev_w_gate, ev_w_up, ev_w_down, od_attn_norm, od_w_qkv, od_w_o, od_ffn_norm, od_w_router, od_w_gate, od_w_up, od_w_down, rel_bias, final_norm):
    b, s, dm = x.shape
    half = MLA_ROPE // 2
    x2 = x.reshape(b * s, dm)
    cos_t, sin_t = _rope_tables(s)

    qn, qr, kn, kr, va, sbq, sbk, sbv = _l0_proj(x2, ev_attn_norm[0], ev_w_in[0], ev_q_norm[0], ev_w_uq[0],
                                                 ev_kv_norm[0], ev_w_ukv[0], cos_t, sin_t, s)
    q_a = jnp.concatenate([qn.reshape(b, s, MLA_HEADS, MLA_NOPE),
                           qr[:, :LANES].reshape(b, s, MLA_HEADS, half),
                           qr[:, LANES:].reshape(b, s, MLA_HEADS, half)], axis=-1).transpose(0, 2, 1, 3)
    k_rope = jnp.concatenate([kr[:, :half], kr[:, LANES:LANES + half]], axis=-1).reshape(b, s, 1, MLA_ROPE)
    k_a = jnp.concatenate([kn.reshape(b, s, MLA_HEADS, MLA_NOPE),
                           jnp.broadcast_to(k_rope, (b, s, MLA_HEADS, MLA_ROPE))], axis=-1).transpose(0, 2, 1, 3)
    o_a = _mla_attention(q_a, k_a, _to_heads(va, b, s, MLA_HEADS))
    o_b = _sb_attention(_to_heads(sbq, b, s, SB_HEADS), _to_heads(sbk, b, s, SB_HEADS),
                        _to_heads(sbv, b, s, SB_HEADS))
    mixed = jnp.concatenate([_merge_heads(o_a), _merge_heads(o_b)], axis=-1)
    x2 = _out_proj(mixed, ev_w_o[0], x2)
    x2 = _swiglu(x2, ev_ffn_norm[0], ev_w_gate[0], ev_w_up[0], ev_w_down[0])

    q, k, v = _l1_proj(x2, od_attn_norm[0], od_w_qkv[0])
    o_c = _moba_attention(_to_heads(q, b, s, MOBA_HEADS), _to_heads(k, b, s, MOBA_HEADS),
                          _to_heads(v, b, s, MOBA_HEADS), rel_bias)
    x2 = _out_proj(_merge_heads(o_c), od_w_o[0], x2)
    out = _moe_final(x2, od_ffn_norm[0], od_w_router[0], od_w_gate[0], od_w_up[0], od_w_down[0], final_norm)
    return out.reshape(b, s, dm)
```

```python
import functools
import math

import jax
import jax.numpy as jnp
import numpy as np
from jax import lax
from jax.experimental import pallas as pl
from jax.experimental.pallas import tpu as pltpu

MLA_HEADS = 8
MLA_NOPE = 64
MLA_ROPE = 32
MLA_V = 64
Q_LORA = 384
KV_LORA = 256
ROPE_THETA = 10000.0
SB_HEADS = 8
SB_HEAD_DIM = 64
MOBA_HEADS = 16
MOBA_HEAD_DIM = 64
MOBA_BLOCK = 256
MOBA_TOPK = 3
REL_BUCKETS = 32
REL_MAX_EXACT = REL_BUCKETS // 2
REL_MAX_DIST = 128
N_EXPERTS = 8
TOP_K = 2
RMS_EPS = 1e-6

LANES = 128
VMEM_LIMIT = 56 * 1024 * 1024
ROW_TILE = 512
MOE_ROW_TILE = 1024
MLA_TILE = 2048
MLA_SUB = 256
MOBA_FAR = 8
MOBA_SUB = 128
MOBA_MASK = 2.0 ** 100
SB_TILE = 256
SB_UNDERFLOW = -128.0
LOG2E = 1.4426950408889634

BF16 = jnp.bfloat16
F32 = jnp.float32


def _params(*sem):
    return pltpu.CompilerParams(dimension_semantics=sem, vmem_limit_bytes=VMEM_LIMIT)


def _rms(x, g):
    return x * lax.rsqrt(jnp.mean(x * x, axis=-1, keepdims=True) + RMS_EPS) * g


def _dot(a, b):
    return jnp.dot(a, b, preferred_element_type=F32)


def _dot_nt(a, b):
    return lax.dot_general(a, b, (((1,), (1,)), ((), ())), preferred_element_type=F32)


def _full(shape):
    return pl.BlockSpec(shape, lambda *_: (0,) * len(shape))


def _l0_proj_kernel(x_ref, g_ref, wcq_ref, wckv_ref, wkr_ref, wsb_ref, qn_ref, wuqn_ref, wuqr_ref,
                    kvn_ref, wukk_ref, wukv_ref, cos_ref, sin_ref,
                    qnope_ref, qrope_ref, knope_ref, krope_ref, va_ref, sbq_ref, sbk_ref, sbv_ref):
    hn = _rms(x_ref[...], g_ref[...]).astype(BF16)
    c = cos_ref[...]
    s = sin_ref[...]

    def rope(t):
        t1 = t[:, :LANES]
        t2 = t[:, LANES:]
        return t1 * c - t2 * s, t2 * c + t1 * s

    cq = _rms(_dot(hn, wcq_ref[...]), qn_ref[...]).astype(BF16)
    qnope_ref[...] = _dot(cq, wuqn_ref[...]).astype(BF16)
    r1, r2 = rope(_dot(cq, wuqr_ref[...]))
    qrope_ref[:, :LANES] = r1.astype(BF16)
    qrope_ref[:, LANES:] = r2.astype(BF16)

    ckv = _rms(_dot(hn, wckv_ref[...]), kvn_ref[...]).astype(BF16)
    knope_ref[...] = _dot(ckv, wukk_ref[...]).astype(BF16)
    va_ref[...] = _dot(ckv, wukv_ref[...]).astype(BF16)

    k1, k2 = rope(_dot(hn, wkr_ref[...]))
    krope_ref[:, :LANES] = k1.astype(BF16)
    krope_ref[:, LANES:] = k2.astype(BF16)

    d = SB_HEADS * SB_HEAD_DIM
    sb = _dot(hn, wsb_ref[...])
    sbq_ref[...] = (sb[:, :d] * (SB_HEAD_DIM ** -0.5)).astype(BF16)
    sbk_ref[...] = sb[:, d:2 * d].astype(BF16)
    sbv_ref[...] = sb[:, 2 * d:].astype(BF16)


def _l0_proj(x2, g, w_in, q_norm, w_uq, kv_norm, w_ukv, cos_t, sin_t, seq):
    rows, dm = x2.shape
    tm = min(ROW_TILE, seq)
    half = MLA_ROPE // 2
    o_kr = Q_LORA + KV_LORA
    o_sb = o_kr + MLA_ROPE
    zpad = jnp.zeros((dm, LANES - half), F32)
    w_cq = w_in[:, :Q_LORA].astype(BF16)
    w_ckv = w_in[:, Q_LORA:o_kr].astype(BF16)
    w_kr = jnp.concatenate([w_in[:, o_kr:o_kr + half], zpad, w_in[:, o_kr + half:o_sb], zpad],
                           axis=1).astype(BF16)
    w_sb = w_in[:, o_sb:].astype(BF16)
    uq = w_uq.reshape(Q_LORA, MLA_HEADS, MLA_NOPE + MLA_ROPE)
    w_uqn = uq[:, :, :MLA_NOPE].reshape(Q_LORA, -1).astype(BF16)
    w_uqr = jnp.concatenate([uq[:, :, MLA_NOPE:MLA_NOPE + half].reshape(Q_LORA, -1),
                             uq[:, :, MLA_NOPE + half:].reshape(Q_LORA, -1)], axis=1).astype(BF16)
    ukv = w_ukv.reshape(KV_LORA, MLA_HEADS, MLA_NOPE + MLA_V)
    w_ukk = ukv[:, :, :MLA_NOPE].reshape(KV_LORA, -1).astype(BF16)
    w_ukv2 = ukv[:, :, MLA_NOPE:].reshape(KV_LORA, -1).astype(BF16)
    n_s = seq // tm
    row = lambda n: pl.BlockSpec((tm, n), lambda i: (i, 0))
    tab = pl.BlockSpec((tm, LANES), lambda i: (i % n_s, 0))
    d_a = MLA_HEADS * MLA_NOPE
    d_b = SB_HEADS * SB_HEAD_DIM
    out_widths = [d_a, 2 * LANES, d_a, 2 * LANES, MLA_HEADS * MLA_V, d_b, d_b, d_b]
    return pl.pallas_call(
        _l0_proj_kernel,
        grid=(rows // tm,),
        in_specs=[row(dm), _full((1, dm)), _full(w_cq.shape), _full(w_ckv.shape), _full(w_kr.shape),
                  _full(w_sb.shape), _full((1, Q_LORA)), _full(w_uqn.shape), _full(w_uqr.shape),
                  _full((1, KV_LORA)), _full(w_ukk.shape), _full(w_ukv2.shape), tab, tab],
        out_specs=[row(n) for n in out_widths],
        out_shape=[jax.ShapeDtypeStruct((rows, n), BF16) for n in out_widths],
        compiler_params=_params("parallel"),
    )(x2, g.reshape(1, dm), w_cq, w_ckv, w_kr, w_sb, q_norm.reshape(1, -1), w_uqn, w_uqr,
      kv_norm.reshape(1, -1), w_ukk, w_ukv2, cos_t, sin_t)


def _mla_kernel(q_ref, k_ref, v_ref, o_ref, m_sc, l_sc, acc_sc, *, t, sub, c):
    qi = pl.program_id(2)
    m_sc[...] = jnp.full_like(m_sc, -jnp.inf)
    l_sc[...] = jnp.zeros_like(l_sc)
    acc_sc[...] = jnp.zeros_like(acc_sc)

    def tile(j, diagonal):
        off = pl.multiple_of(j * t, t)
        k = k_ref[pl.ds(off, t), :]
        v = v_ref[pl.ds(off, t), :]
        m_all, l_all, acc_all = m_sc[...], l_sc[...], acc_sc[...]
        m_out, l_out, acc_out = [], [], []
        widths = [r0 + sub if diagonal else t for r0 in range(0, t, sub)]
        logits = [_dot_nt(q_ref[pl.ds(r0, sub), :], k[:w]) for r0, w in zip(range(0, t, sub), widths)]
        for r0, width, s in zip(range(0, t, sub), widths, logits):
            if diagonal:
                r = lax.broadcasted_iota(jnp.int32, (sub, width), 0) + r0
                cc = lax.broadcasted_iota(jnp.int32, (sub, width), 1)
                s = jnp.where(cc <= r, s, -jnp.inf)
            m_prev = m_all[r0:r0 + sub]
            m_new = jnp.maximum(m_prev, jnp.max(s, axis=-1, keepdims=True))
            alpha = jnp.exp2((m_prev - m_new) * c)
            p = jnp.exp2((s - m_new) * c)
            l_out.append(alpha * l_all[r0:r0 + sub] + jnp.sum(p, axis=-1, keepdims=True))
            acc_out.append(alpha * acc_all[r0:r0 + sub] + _dot(p.astype(BF16), v[:width]))
            m_out.append(m_new)
        m_sc[...] = jnp.concatenate(m_out, axis=0)
        l_sc[...] = jnp.concatenate(l_out, axis=0)
        acc_sc[...] = jnp.concatenate(acc_out, axis=0)

    def body(j, carry):
        tile(j, False)
        return carry

    lax.fori_loop(0, qi, body, 0)
    tile(qi, True)
    o_ref[...] = (acc_sc[...] / l_sc[...]).astype(o_ref.dtype)


def _mla_attention(q, k, v):
    b, h, s, dq = q.shape
    dv = v.shape[-1]
    t = min(MLA_TILE, s)
    c = (MLA_NOPE + MLA_ROPE) ** -0.5 * LOG2E
    return pl.pallas_call(
        functools.partial(_mla_kernel, t=t, sub=min(MLA_SUB, t), c=c),
        grid=(b, h, s // t),
        in_specs=[pl.BlockSpec((None, None, t, dq), lambda bi, hi, i: (bi, hi, i, 0)),
                  pl.BlockSpec((None, None, s, dq), lambda bi, hi, i: (bi, hi, 0, 0)),
                  pl.BlockSpec((None, None, s, dv), lambda bi, hi, i: (bi, hi, 0, 0))],
        out_specs=pl.BlockSpec((None, None, t, dv), lambda bi, hi, i: (bi, hi, i, 0)),
        out_shape=jax.ShapeDtypeStruct((b, h, s, dv), BF16),
        scratch_shapes=[pltpu.VMEM((t, 1), F32), pltpu.VMEM((t, 1), F32), pltpu.VMEM((t, dv), F32)],
        compiler_params=_params("parallel", "parallel", "arbitrary"),
    )(q, k, v)


def _split3(x):
    hi = x.astype(BF16)
    r = x - hi.astype(F32)
    mid = r.astype(BF16)
    lo = (r - mid.astype(F32)).astype(BF16)
    return hi, mid, lo


def _sb_kernel(q_ref, k_ref, v_ref, o_ref, acc_sc, run_sc, *, t):
    qi = pl.program_id(2)
    q = q_ref[...]
    acc_sc[...] = jnp.zeros_like(acc_sc)
    run_sc[...] = jnp.zeros_like(run_sc)
    r = lax.broadcasted_iota(jnp.int32, (t, t), 0)
    cc = lax.broadcasted_iota(jnp.int32, (t, t), 1)
    later = (r > cc).astype(BF16)

    def tile(j, diagonal):
        off = pl.multiple_of(j * t, t)
        z = _dot_nt(q, k_ref[pl.ds(off, t), :])
        log_fail = jnp.minimum(-z, 0.0) - jnp.log(1.0 + jnp.exp(-jnp.abs(z)))
        log_hit = log_fail + z
        if diagonal:
            past = cc < r
            log_fail = jnp.where(past, log_fail, 0.0)
        hi, mid, lo = _split3(log_fail)
        within = _dot(hi, later) + _dot(mid, later) + _dot(lo, later)
        w = jnp.exp(log_hit + within + run_sc[...])
        if diagonal:
            w = jnp.where(past, w, 0.0)
        acc_sc[...] += _dot(w.astype(BF16), v_ref[pl.ds(off, t), :])
        run_sc[...] += jnp.sum(log_fail, axis=-1, keepdims=True)

    tile(qi, True)

    def cond(carry):
        j, worst = carry
        return jnp.logical_and(j >= 0, worst > SB_UNDERFLOW)

    def body(carry):
        j, _ = carry
        tile(j, False)
        return j - 1, jnp.max(run_sc[...])

    lax.while_loop(cond, body, (qi - 1, jnp.max(run_sc[...])))
    o_ref[...] = acc_sc[...].astype(o_ref.dtype)


def _sb_attention(q, k, v):
    b, h, s, d = q.shape
    t = min(SB_TILE, s)
    return pl.pallas_call(
        functools.partial(_sb_kernel, t=t),
        grid=(b, h, s // t),
        in_specs=[pl.BlockSpec((None, None, t, d), lambda bi, hi, i: (bi, hi, i, 0)),
                  pl.BlockSpec((None, None, s, d), lambda bi, hi, i: (bi, hi, 0, 0)),
                  pl.BlockSpec((None, None, s, d), lambda bi, hi, i: (bi, hi, 0, 0))],
        out_specs=pl.BlockSpec((None, None, t, d), lambda bi, hi, i: (bi, hi, i, 0)),
        out_shape=jax.ShapeDtypeStruct((b, h, s, d), BF16),
        scratch_shapes=[pltpu.VMEM((t, d), F32), pltpu.VMEM((t, 1), F32)],
        compiler_params=_params("parallel", "parallel", "arbitrary"),
    )(q, k, v)


def _moba_kernel(q_ref, k_ref, v_ref, bown_ref, bprev_ref, bfar_ref, o_ref,
                 kmean_sc, qaug_sc, m_sc, l_sc, acc_sc, *, nb, sub, far):
    blk = MOBA_BLOCK
    d = MOBA_HEAD_DIM
    i = pl.program_id(2)

    @pl.when(i == 0)
    def _():
        kf = k_ref[:, pl.ds(0, d)].astype(F32).reshape(nb, blk, d)
        kmean_sc[...] = jnp.sum(kf, axis=1) * (1.0 / blk)

    q = q_ref[...]
    k1, k2, k3 = _split3(kmean_sc[...])
    gate = _dot_nt(q, k1) + _dot_nt(q, k2) + _dot_nt(q, k3)
    lane = lax.broadcasted_iota(jnp.int32, (blk, nb), 1)
    gate = jnp.where(lane < i, gate, -jnp.inf)
    sel = jnp.zeros((blk, nb), F32)
    for _ in range(MOBA_TOPK):
        mx = jnp.max(gate, axis=-1, keepdims=True)
        is_max = jnp.logical_and(gate == mx, mx > -jnp.inf)
        idx = jnp.min(jnp.where(is_max, lane, nb), axis=-1, keepdims=True)
        pick = lane == idx
        sel = jnp.where(pick, 1.0, sel)
        gate = jnp.where(pick, -jnp.inf, gate)
    qaug_sc[:, pl.ds(0, d)] = q
    qaug_sc[:, pl.ds(d, nb)] = (1.0 - sel).astype(BF16)

    row_tiles = [pl.ds(r0, sub) for r0 in range(0, blk, sub)]

    off_own = pl.multiple_of(i * blk, blk)
    k_own = k_ref[pl.ds(off_own, blk), pl.ds(0, d)]
    v_own = v_ref[pl.ds(off_own, blk), :]
    for rows in row_tiles:
        width = rows.start + sub
        r = lax.broadcasted_iota(jnp.int32, (sub, width), 0) + rows.start
        cc = lax.broadcasted_iota(jnp.int32, (sub, width), 1)
        s = _dot_nt(q_ref[rows, :], k_own[:width]) + bown_ref[rows, pl.ds(0, width)]
        s = jnp.where(cc <= r, s, -jnp.inf)
        m0 = jnp.max(s, axis=-1, keepdims=True)
        p = jnp.exp(s - m0)
        m_sc[rows, :] = m0
        l_sc[rows, :] = jnp.sum(p, axis=-1, keepdims=True)
        acc_sc[rows, :] = _dot(p.astype(BF16), v_own[:width])

    def update(logits, v):
        m_all, l_all, acc_all = m_sc[...], l_sc[...], acc_sc[...]
        m_out, l_out, acc_out = [], [], []
        for rows, s in zip(row_tiles, logits):
            lo, hi = rows.start, rows.start + sub
            m_prev = m_all[lo:hi]
            m_new = jnp.maximum(m_prev, jnp.max(s, axis=-1, keepdims=True))
            alpha = jnp.exp(m_prev - m_new)
            p = jnp.exp(s - m_new)
            l_out.append(alpha * l_all[lo:hi] + jnp.sum(p, axis=-1, keepdims=True))
            acc_out.append(alpha * acc_all[lo:hi] + _dot(p.astype(BF16), v))
            m_out.append(m_new)
        m_sc[...] = jnp.concatenate(m_out, axis=0)
        l_sc[...] = jnp.concatenate(l_out, axis=0)
        acc_sc[...] = jnp.concatenate(acc_out, axis=0)

    def far_group(g, carry):
        off = pl.multiple_of(g * (far * blk), far * blk)
        k = k_ref[pl.ds(off, far * blk), :]
        v = v_ref[pl.ds(off, far * blk), :]
        update([_dot_nt(qaug_sc[rows, :], k) + bfar_ref[...] for rows in row_tiles], v)
        return carry

    n_groups = (jnp.maximum(i, 1) - 1) // far
    lax.fori_loop(0, n_groups, far_group, 0)

    tail = i - n_groups * far
    for w in range(1, far + 1):
        @pl.when(tail == w)
        def _(w=w):
            off = pl.multiple_of((i - w) * blk, blk)
            k = k_ref[pl.ds(off, w * blk), :]
            v = v_ref[pl.ds(off, w * blk), :]
            logits = []
            for rows in row_tiles:
                s = _dot_nt(qaug_sc[rows, :], k)
                cut = (w - 1) * blk
                prev = s[:, cut:] + bprev_ref[rows, :]
                logits.append(prev if w == 1 else
                              jnp.concatenate([s[:, :cut] + bfar_ref[:, pl.ds(0, cut)], prev], axis=1))
            update(logits, v)

    o_ref[...] = (acc_sc[...] / l_sc[...]).astype(o_ref.dtype)


def _t5_bucket(dist):
    n = jnp.maximum(dist, 0)
    nf = jnp.maximum(n, 1).astype(F32)
    large = REL_MAX_EXACT + (jnp.log(nf / REL_MAX_EXACT) / math.log(REL_MAX_DIST / REL_MAX_EXACT)
                             * (REL_BUCKETS - REL_MAX_EXACT)).astype(jnp.int32)
    large = jnp.minimum(large, REL_BUCKETS - 1)
    return jnp.where(n < REL_MAX_EXACT, n, large)


def _moba_attention(q, k, v, rel_bias):
    b, h, s, d = q.shape
    blk = MOBA_BLOCK
    assert s % blk == 0 and d == MOBA_HEAD_DIM
    assert 2 * blk > REL_MAX_DIST
    nb = s // blk
    far = min(MOBA_FAR, nb)
    offs = jnp.arange(blk)
    d_own = offs[:, None] - offs[None, :]
    b_own = rel_bias[:, _t5_bucket(d_own)]
    b_prev = rel_bias[:, _t5_bucket(d_own + blk)]
    b_far = jnp.broadcast_to(rel_bias[:, REL_BUCKETS - 1][:, None, None], (h, 1, far * blk))
    own_block = (jnp.arange(s)[:, None] // blk) == jnp.arange(nb)[None, :]
    marks = jnp.where(own_block, -MOBA_MASK, 0.0).astype(BF16)
    k_aug = jnp.concatenate([k, jnp.broadcast_to(marks, (b, h, s, nb))], axis=-1)
    return pl.pallas_call(
        functools.partial(_moba_kernel, nb=nb, sub=MOBA_SUB, far=far),
        grid=(b, h, nb),
        in_specs=[pl.BlockSpec((None, None, blk, d), lambda bi, hi, i: (bi, hi, i, 0)),
                  pl.BlockSpec((None, None, s, d + nb), lambda bi, hi, i: (bi, hi, 0, 0)),
                  pl.BlockSpec((None, None, s, d), lambda bi, hi, i: (bi, hi, 0, 0)),
                  pl.BlockSpec((None, blk, blk), lambda bi, hi, i: (hi, 0, 0)),
                  pl.BlockSpec((None, blk, blk), lambda bi, hi, i: (hi, 0, 0)),
                  pl.BlockSpec((None, 1, far * blk), lambda bi, hi, i: (hi, 0, 0))],
        out_specs=pl.BlockSpec((None, None, blk, d), lambda bi, hi, i: (bi, hi, i, 0)),
        out_shape=jax.ShapeDtypeStruct((b, h, s, d), BF16),
        scratch_shapes=[pltpu.VMEM((nb, d), F32), pltpu.VMEM((blk, d + nb), BF16), pltpu.VMEM((blk, 1), F32),
                        pltpu.VMEM((blk, 1), F32), pltpu.VMEM((blk, d), F32)],
        compiler_params=_params("parallel", "parallel", "arbitrary"),
    )(q, k_aug, v, b_own, b_prev, b_far)


def _norm_proj_kernel(x_ref, g_ref, w_ref, q_ref, k_ref, v_ref, *, width):
    hn = _rms(x_ref[...], g_ref[...]).astype(BF16)
    y = _dot(hn, w_ref[...])
    q_ref[...] = (y[:, :width] * (MOBA_HEAD_DIM ** -0.5)).astype(BF16)
    k_ref[...] = y[:, width:2 * width].astype(BF16)
    v_ref[...] = y[:, 2 * width:].astype(BF16)


def _l1_proj(x2, g, w_qkv):
    rows, dm = x2.shape
    tm = min(ROW_TILE, rows)
    width = w_qkv.shape[1] // 3
    row = lambda n: pl.BlockSpec((tm, n), lambda i: (i, 0))
    return pl.pallas_call(
        functools.partial(_norm_proj_kernel, width=width),
        grid=(rows // tm,),
        in_specs=[row(dm), _full((1, dm)), _full(w_qkv.shape)],
        out_specs=[row(width)] * 3,
        out_shape=[jax.ShapeDtypeStruct((rows, width), BF16)] * 3,
        compiler_params=_params("parallel"),
    )(x2, g.reshape(1, dm), w_qkv.astype(BF16))


def _out_proj_kernel(a_ref, w_ref, x_ref, o_ref):
    o_ref[...] = x_ref[...] + _dot(a_ref[...], w_ref[...])


def _out_proj(a, w, x2):
    rows, dm = x2.shape
    tm = min(ROW_TILE, rows)
    return pl.pallas_call(
        _out_proj_kernel,
        grid=(rows // tm,),
        in_specs=[pl.BlockSpec((tm, a.shape[1]), lambda i: (i, 0)), _full(w.shape),
                  pl.BlockSpec((tm, dm), lambda i: (i, 0))],
        out_specs=pl.BlockSpec((tm, dm), lambda i: (i, 0)),
        out_shape=jax.ShapeDtypeStruct((rows, dm), F32),
        compiler_params=_params("parallel"),
    )(a, w.astype(BF16), x2)


def _silu(x):
    return x / (1.0 + jnp.exp(-x))


def _swiglu_kernel(x_ref, g_ref, wg_ref, wu_ref, wd_ref, o_ref, *, chunk):
    x = x_ref[...]
    hn = _rms(x, g_ref[...]).astype(BF16)
    o_ref[...] = x
    for c0 in range(0, wg_ref.shape[1], chunk):
        a = _silu(_dot(hn, wg_ref[:, c0:c0 + chunk])) * _dot(hn, wu_ref[:, c0:c0 + chunk])
        o_ref[...] += _dot(a.astype(BF16), wd_ref[c0:c0 + chunk, :])


def _swiglu(x2, g, w_gate, w_up, w_down):
    rows, dm = x2.shape
    tm = min(ROW_TILE, rows)
    return pl.pallas_call(
        functools.partial(_swiglu_kernel, chunk=512),
        grid=(rows // tm,),
        in_specs=[pl.BlockSpec((tm, dm), lambda i: (i, 0)), _full((1, dm)), _full(w_gate.shape),
                  _full(w_up.shape), _full(w_down.shape)],
        out_specs=pl.BlockSpec((tm, dm), lambda i: (i, 0)),
        out_shape=jax.ShapeDtypeStruct((rows, dm), F32),
        compiler_params=_params("parallel"),
    )(x2, g.reshape(1, dm), w_gate.astype(BF16), w_up.astype(BF16), w_down.astype(BF16))


def _moe_kernel(x_ref, g_ref, wr_ref, wg_ref, wu_ref, wd_ref, gf_ref, o_ref, hn_sc, gate_sc, acc_sc):
    e = pl.program_id(1)
    lane = lax.broadcasted_iota(jnp.int32, gate_sc.shape, 1)

    @pl.when(e == 0)
    def _():
        h = _rms(x_ref[...], g_ref[...])
        hn_sc[...] = h.astype(BF16)
        logits = jnp.dot(h, wr_ref[...], preferred_element_type=F32, precision=lax.Precision.HIGHEST)
        logits = jnp.where(lane < N_EXPERTS, logits, -jnp.inf)
        v1 = jnp.max(logits, axis=-1, keepdims=True)
        i1 = jnp.min(jnp.where(logits == v1, lane, LANES), axis=-1, keepdims=True)
        rest = jnp.where(lane == i1, -jnp.inf, logits)
        v2 = jnp.max(rest, axis=-1, keepdims=True)
        i2 = jnp.min(jnp.where(rest == v2, lane, LANES), axis=-1, keepdims=True)
        p2 = jnp.exp(v2 - v1)
        w1 = 1.0 / (1.0 + p2)
        gate_sc[...] = jnp.where(lane == i1, w1, 0.0) + jnp.where(lane == i2, p2 * w1, 0.0)
        acc_sc[...] = jnp.zeros_like(acc_sc)

    hn = hn_sc[...]
    gate = jnp.sum(jnp.where(lane == e, gate_sc[...], 0.0), axis=-1, keepdims=True)
    a = _silu(_dot(hn, wg_ref[...])) * _dot(hn, wu_ref[...]) * gate
    acc_sc[...] += _dot(a.astype(BF16), wd_ref[...])

    @pl.when(e == pl.num_programs(1) - 1)
    def _():
        o_ref[...] = _rms(x_ref[...] + acc_sc[...], gf_ref[...])


def _moe_final(x2, g, w_router, w_gate, w_up, w_down, g_final):
    rows, dm = x2.shape
    tm = min(MOE_ROW_TILE, rows)
    ne, _, dff = w_gate.shape
    w_r = jnp.zeros((dm, LANES), F32).at[:, :ne].set(w_router)
    return pl.pallas_call(
        _moe_kernel,
        grid=(rows // tm, ne),
        in_specs=[pl.BlockSpec((tm, dm), lambda i, e: (i, 0)),
                  pl.BlockSpec((1, dm), lambda i, e: (0, 0)),
                  pl.BlockSpec((dm, LANES), lambda i, e: (0, 0)),
                  pl.BlockSpec((None, dm, dff), lambda i, e: (e, 0, 0)),
                  pl.BlockSpec((None, dm, dff), lambda i, e: (e, 0, 0)),
                  pl.BlockSpec((None, dff, dm), lambda i, e: (e, 0, 0)),
                  pl.BlockSpec((1, dm), lambda i, e: (0, 0))],
        out_specs=pl.BlockSpec((tm, dm), lambda i, e: (i, 0)),
        out_shape=jax.ShapeDtypeStruct((rows, dm), F32),
        scratch_shapes=[pltpu.VMEM((tm, dm), BF16), pltpu.VMEM((tm, LANES), F32), pltpu.VMEM((tm, dm), F32)],
        compiler_params=_params("parallel", "arbitrary"),
    )(x2, g.reshape(1, dm), w_r, w_gate.astype(BF16), w_up.astype(BF16), w_down.astype(BF16),
      g_final.reshape(1, dm))


def _to_heads(t, b, s, n_heads):
    return t.reshape(b, s, n_heads, -1).transpose(0, 2, 1, 3)


def _merge_heads(t):
    b, h, s, d = t.shape
    return t.transpose(0, 2, 1, 3).reshape(b * s, h * d)


def _rope_tables(seq):
    half = MLA_ROPE // 2
    inv = 1.0 / (ROPE_THETA ** (jnp.arange(0, MLA_ROPE, 2, dtype=F32) / MLA_ROPE))
    ang = jnp.arange(seq, dtype=F32)[:, None] * inv[None, :]
    reps = LANES // half
    return jnp.tile(jnp.cos(ang), (1, reps)), jnp.tile(jnp.sin(ang), (1, reps))


def kernel(x, ev_attn_norm, ev_w_in, ev_q_norm, ev_w_uq, ev_kv_norm, ev_w_ukv, ev_w_o, ev_ffn_norm, ev_w_gate, ev_w_up, ev_w_down, od_attn_norm, od_w_qkv, od_w_o, od_ffn_norm, od_w_router, od_w_gate, od_w_up, od_w_down, rel_bias, final_norm):
    b, s, dm = x.shape
    half = MLA_ROPE // 2
    x2 = x.reshape(b * s, dm)
    cos_t, sin_t = _rope_tables(s)

    qn, qr, kn, kr, va, sbq, sbk, sbv = _l0_proj(x2, ev_attn_norm[0], ev_w_in[0], ev_q_norm[0], ev_w_uq[0],
                                                 ev_kv_norm[0], ev_w_ukv[0], cos_t, sin_t, s)
    q_a = jnp.concatenate([qn.reshape(b, s, MLA_HEADS, MLA_NOPE),
                           qr[:, :LANES].reshape(b, s, MLA_HEADS, half),
                           qr[:, LANES:].reshape(b, s, MLA_HEADS, half)], axis=-1).transpose(0, 2, 1, 3)
    k_rope = jnp.concatenate([kr[:, :half], kr[:, LANES:LANES + half]], axis=-1).reshape(b, s, 1, MLA_ROPE)
    k_a = jnp.concatenate([kn.reshape(b, s, MLA_HEADS, MLA_NOPE),
                           jnp.broadcast_to(k_rope, (b, s, MLA_HEADS, MLA_ROPE))], axis=-1).transpose(0, 2, 1, 3)
    o_a = _mla_attention(q_a, k_a, _to_heads(va, b, s, MLA_HEADS))
    o_b = _sb_attention(_to_heads(sbq, b, s, SB_HEADS), _to_heads(sbk, b, s, SB_HEADS),
                        _to_heads(sbv, b, s, SB_HEADS))
    mixed = jnp.concatenate([_merge_heads(o_a), _merge_heads(o_b)], axis=-1)
    x2 = _out_proj(mixed, ev_w_o[0], x2)
    x2 = _swiglu(x2, ev_ffn_norm[0], ev_w_gate[0], ev_w_up[0], ev_w_down[0])

    q, k, v = _l1_proj(x2, od_attn_norm[0], od_w_qkv[0])
    o_c = _moba_attention(_to_heads(q, b, s, MOBA_HEADS), _to_heads(k, b, s, MOBA_HEADS),
                          _to_heads(v, b, s, MOBA_HEADS), rel_bias)
    x2 = _out_proj(_merge_heads(o_c), od_w_o[0], x2)
    out = _moe_final(x2, od_ffn_norm[0], od_w_router[0], od_w_gate[0], od_w_up[0], od_w_down[0], final_norm)
    return out.reshape(b, s, dm)
```

```python
import functools
import math

import jax
import jax.numpy as jnp
import numpy as np
from jax import lax
from jax.experimental import pallas as pl
from jax.experimental.pallas import tpu as pltpu

MLA_HEADS = 8
MLA_NOPE = 64
MLA_ROPE = 32
MLA_V = 64
Q_LORA = 384
KV_LORA = 256
ROPE_THETA = 10000.0
SB_HEADS = 8
SB_HEAD_DIM = 64
MOBA_HEADS = 16
MOBA_HEAD_DIM = 64
MOBA_BLOCK = 256
MOBA_TOPK = 3
REL_BUCKETS = 32
REL_MAX_EXACT = REL_BUCKETS // 2
REL_MAX_DIST = 128
N_EXPERTS = 8
TOP_K = 2
RMS_EPS = 1e-6

LANES = 128
VMEM_LIMIT = 56 * 1024 * 1024
ROW_TILE = 512
MOE_ROW_TILE = 1024
MLA_TILE = 2048
MLA_SUB = 256
MOBA_FAR = 8
MOBA_SUB = 128
MOBA_MASK = 2.0 ** 100
SB_TILE = 256
SB_UNDERFLOW = -128.0
LOG2E = 1.4426950408889634

BF16 = jnp.bfloat16
F32 = jnp.float32


def _params(*sem):
    return pltpu.CompilerParams(dimension_semantics=sem, vmem_limit_bytes=VMEM_LIMIT)


def _rms(x, g):
    return x * lax.rsqrt(jnp.mean(x * x, axis=-1, keepdims=True) + RMS_EPS) * g


def _dot(a, b):
    return jnp.dot(a, b, preferred_element_type=F32)


def _dot_nt(a, b):
    return lax.dot_general(a, b, (((1,), (1,)), ((), ())), preferred_element_type=F32)


def _full(shape):
    return pl.BlockSpec(shape, lambda *_: (0,) * len(shape))


def _heads(y, h, d):
    return y[:, h * d:(h + 1) * d]


def _l0_proj_kernel(x_ref, g_ref, wcq_ref, wckv_ref, wkr_ref, wsb_ref, qn_ref, wuqn_ref, wuqr_ref,
                    kvn_ref, wukk_ref, wukv_ref, cos_ref, sin_ref,
                    qa_ref, ka_ref, va_ref, sbq_ref, sbk_ref, sbv_ref):
    hn = _rms(x_ref[...], g_ref[...]).astype(BF16)
    c = cos_ref[...]
    s = sin_ref[...]
    half = MLA_ROPE // 2

    def rope(t):
        t1 = t[:, :LANES]
        t2 = t[:, LANES:]
        return t1 * c - t2 * s, t2 * c + t1 * s

    cq = _rms(_dot(hn, wcq_ref[...]), qn_ref[...]).astype(BF16)
    qn = _dot(cq, wuqn_ref[...])
    r1, r2 = rope(_dot(cq, wuqr_ref[...]))
    ckv = _rms(_dot(hn, wckv_ref[...]), kvn_ref[...]).astype(BF16)
    kn = _dot(ckv, wukk_ref[...])
    va = _dot(ckv, wukv_ref[...])
    k1, k2 = rope(_dot(hn, wkr_ref[...]))
    k_rope = jnp.concatenate([k1[:, :half], k2[:, :half]], axis=-1).astype(BF16)
    for h in range(MLA_HEADS):
        qa_ref[h] = jnp.concatenate([_heads(qn, h, MLA_NOPE), _heads(r1, h, half), _heads(r2, h, half)],
                                    axis=-1).astype(BF16)
        ka_ref[h] = jnp.concatenate([_heads(kn, h, MLA_NOPE).astype(BF16), k_rope], axis=-1)
        va_ref[h] = _heads(va, h, MLA_V).astype(BF16)

    d = SB_HEADS * SB_HEAD_DIM
    sb = _dot(hn, wsb_ref[...])
    for h in range(SB_HEADS):
        sbq_ref[h] = (_heads(sb[:, :d], h, SB_HEAD_DIM) * (SB_HEAD_DIM ** -0.5)).astype(BF16)
        sbk_ref[h] = _heads(sb[:, d:2 * d], h, SB_HEAD_DIM).astype(BF16)
        sbv_ref[h] = _heads(sb[:, 2 * d:], h, SB_HEAD_DIM).astype(BF16)


def _l0_proj(x, g, w_in, q_norm, w_uq, kv_norm, w_ukv, cos_t, sin_t):
    b, seq, dm = x.shape
    tm = min(ROW_TILE, seq)
    half = MLA_ROPE // 2
    o_kr = Q_LORA + KV_LORA
    o_sb = o_kr + MLA_ROPE
    zpad = jnp.zeros((dm, LANES - half), F32)
    w_cq = w_in[:, :Q_LORA].astype(BF16)
    w_ckv = w_in[:, Q_LORA:o_kr].astype(BF16)
    w_kr = jnp.concatenate([w_in[:, o_kr:o_kr + half], zpad, w_in[:, o_kr + half:o_sb], zpad],
                           axis=1).astype(BF16)
    w_sb = w_in[:, o_sb:].astype(BF16)
    uq = w_uq.reshape(Q_LORA, MLA_HEADS, MLA_NOPE + MLA_ROPE)
    w_uqn = uq[:, :, :MLA_NOPE].reshape(Q_LORA, -1).astype(BF16)
    w_uqr = jnp.concatenate([uq[:, :, MLA_NOPE:MLA_NOPE + half].reshape(Q_LORA, -1),
                             uq[:, :, MLA_NOPE + half:].reshape(Q_LORA, -1)], axis=1).astype(BF16)
    ukv = w_ukv.reshape(KV_LORA, MLA_HEADS, MLA_NOPE + MLA_V)
    w_ukk = ukv[:, :, :MLA_NOPE].reshape(KV_LORA, -1).astype(BF16)
    w_ukv2 = ukv[:, :, MLA_NOPE:].reshape(KV_LORA, -1).astype(BF16)
    tab = pl.BlockSpec((tm, LANES), lambda bi, i: (i, 0))
    heads = lambda n, d: pl.BlockSpec((None, n, tm, d), lambda bi, i: (bi, 0, i, 0))
    out_dims = [(MLA_HEADS, MLA_NOPE + MLA_ROPE), (MLA_HEADS, MLA_NOPE + MLA_ROPE), (MLA_HEADS, MLA_V),
                (SB_HEADS, SB_HEAD_DIM), (SB_HEADS, SB_HEAD_DIM), (SB_HEADS, SB_HEAD_DIM)]
    return pl.pallas_call(
        _l0_proj_kernel,
        grid=(b, seq // tm),
        in_specs=[pl.BlockSpec((None, tm, dm), lambda bi, i: (bi, i, 0)), _full((1, dm)), _full(w_cq.shape),
                  _full(w_ckv.shape), _full(w_kr.shape), _full(w_sb.shape), _full((1, Q_LORA)),
                  _full(w_uqn.shape), _full(w_uqr.shape), _full((1, KV_LORA)), _full(w_ukk.shape),
                  _full(w_ukv2.shape), tab, tab],
        out_specs=[heads(n, d) for n, d in out_dims],
        out_shape=[jax.ShapeDtypeStruct((b, n, seq, d), BF16) for n, d in out_dims],
        compiler_params=_params("parallel", "parallel"),
    )(x, g.reshape(1, dm), w_cq, w_ckv, w_kr, w_sb, q_norm.reshape(1, -1), w_uqn, w_uqr,
      kv_norm.reshape(1, -1), w_ukk, w_ukv2, cos_t, sin_t)


def _mla_kernel(q_ref, k_ref, v_ref, o_ref, m_sc, l_sc, acc_sc, *, t, sub, c):
    qi = pl.program_id(2)
    m_sc[...] = jnp.full_like(m_sc, -jnp.inf)
    l_sc[...] = jnp.zeros_like(l_sc)
    acc_sc[...] = jnp.zeros_like(acc_sc)

    def tile(j, diagonal):
        off = pl.multiple_of(j * t, t)
        k = k_ref[pl.ds(off, t), :]
        v = v_ref[pl.ds(off, t), :]
        m_all, l_all, acc_all = m_sc[...], l_sc[...], acc_sc[...]
        m_out, l_out, acc_out = [], [], []
        widths = [r0 + sub if diagonal else t for r0 in range(0, t, sub)]
        logits = [_dot_nt(q_ref[pl.ds(r0, sub), :], k[:w]) for r0, w in zip(range(0, t, sub), widths)]
        for r0, width, s in zip(range(0, t, sub), widths, logits):
            if diagonal:
                r = lax.broadcasted_iota(jnp.int32, (sub, width), 0) + r0
                cc = lax.broadcasted_iota(jnp.int32, (sub, width), 1)
                s = jnp.where(cc <= r, s, -jnp.inf)
            m_prev = m_all[r0:r0 + sub]
            m_new = jnp.maximum(m_prev, jnp.max(s, axis=-1, keepdims=True))
            alpha = jnp.exp2((m_prev - m_new) * c)
            p = jnp.exp2((s - m_new) * c)
            l_out.append(alpha * l_all[r0:r0 + sub] + jnp.sum(p, axis=-1, keepdims=True))
            acc_out.append(alpha * acc_all[r0:r0 + sub] + _dot(p.astype(BF16), v[:width]))
            m_out.append(m_new)
        m_sc[...] = jnp.concatenate(m_out, axis=0)
        l_sc[...] = jnp.concatenate(l_out, axis=0)
        acc_sc[...] = jnp.concatenate(acc_out, axis=0)

    def body(j, carry):
        tile(j, False)
        return carry

    lax.fori_loop(0, qi, body, 0)
    tile(qi, True)
    o_ref[...] = (acc_sc[...] / l_sc[...]).astype(o_ref.dtype)


def _mla_attention(q, k, v):
    b, h, s, dq = q.shape
    dv = v.shape[-1]
    t = min(MLA_TILE, s)
    c = (MLA_NOPE + MLA_ROPE) ** -0.5 * LOG2E
    return pl.pallas_call(
        functools.partial(_mla_kernel, t=t, sub=min(MLA_SUB, t), c=c),
        grid=(b, h, s // t),
        in_specs=[pl.BlockSpec((None, None, t, dq), lambda bi, hi, i: (bi, hi, i, 0)),
                  pl.BlockSpec((None, None, s, dq), lambda bi, hi, i: (bi, hi, 0, 0)),
                  pl.BlockSpec((None, None, s, dv), lambda bi, hi, i: (bi, hi, 0, 0))],
        out_specs=pl.BlockSpec((None, None, t, dv), lambda bi, hi, i: (bi, hi, i, 0)),
        out_shape=jax.ShapeDtypeStruct((b, h, s, dv), BF16),
        scratch_shapes=[pltpu.VMEM((t, 1), F32), pltpu.VMEM((t, 1), F32), pltpu.VMEM((t, dv), F32)],
        compiler_params=_params("parallel", "parallel", "arbitrary"),
    )(q, k, v)


def _split3(x):
    hi = x.astype(BF16)
    r = x - hi.astype(F32)
    mid = r.astype(BF16)
    lo = (r - mid.astype(F32)).astype(BF16)
    return hi, mid, lo


def _sb_kernel(q_ref, k_ref, v_ref, o_ref, acc_sc, run_sc, *, t):
    qi = pl.program_id(2)
    q = q_ref[...]
    acc_sc[...] = jnp.zeros_like(acc_sc)
    run_sc[...] = jnp.zeros_like(run_sc)
    r = lax.broadcasted_iota(jnp.int32, (t, t), 0)
    cc = lax.broadcasted_iota(jnp.int32, (t, t), 1)
    later = (r > cc).astype(BF16)

    past = cc < r

    def scores(j):
        return _dot_nt(q, k_ref[pl.ds(pl.multiple_of(j * t, t), t), :])

    def log_terms(z, diagonal):
        log_fail = jnp.minimum(-z, 0.0) - jnp.log(1.0 + jnp.exp(-jnp.abs(z)))
        log_hit = log_fail + z
        if diagonal:
            log_fail = jnp.where(past, log_fail, 0.0)
        hi, mid, lo = _split3(log_fail)
        within = _dot(hi, later) + _dot(mid, later) + _dot(lo, later)
        return log_hit + within, jnp.sum(log_fail, axis=-1, keepdims=True)

    def weighted(j, log_w, run, diagonal):
        w = jnp.exp(log_w + run)
        if diagonal:
            w = jnp.where(past, w, 0.0)
        return _dot(w.astype(BF16), v_ref[pl.ds(pl.multiple_of(j * t, t), t), :])

    def tile(j, diagonal):
        log_w, total = log_terms(scores(j), diagonal)
        acc_sc[...] += weighted(j, log_w, run_sc[...], diagonal)
        run_sc[...] += total

    @pl.when(qi == 0)
    def _():
        tile(qi, True)

    @pl.when(qi > 0)
    def _():
        z_diag, z_prev = scores(qi), scores(qi - 1)
        lw_diag, total_diag = log_terms(z_diag, True)
        lw_prev, total_prev = log_terms(z_prev, False)
        acc_sc[...] = weighted(qi, lw_diag, 0.0, True) + weighted(qi - 1, lw_prev, total_diag, False)
        run_sc[...] = total_diag + total_prev

    def cond(carry):
        j, worst = carry
        return jnp.logical_and(j >= 0, worst > SB_UNDERFLOW)

    def body(carry):
        j, _ = carry
        tile(j, False)
        return j - 1, jnp.max(run_sc[...])

    lax.while_loop(cond, body, (qi - 2, jnp.max(run_sc[...])))
    o_ref[...] = acc_sc[...].astype(o_ref.dtype)


def _sb_attention(q, k, v):
    b, h, s, d = q.shape
    t = min(SB_TILE, s)
    return pl.pallas_call(
        functools.partial(_sb_kernel, t=t),
        grid=(b, h, s // t),
        in_specs=[pl.BlockSpec((None, None, t, d), lambda bi, hi, i: (bi, hi, i, 0)),
                  pl.BlockSpec((None, None, s, d), lambda bi, hi, i: (bi, hi, 0, 0)),
                  pl.BlockSpec((None, None, s, d), lambda bi, hi, i: (bi, hi, 0, 0))],
        out_specs=pl.BlockSpec((None, None, t, d), lambda bi, hi, i: (bi, hi, i, 0)),
        out_shape=jax.ShapeDtypeStruct((b, h, s, d), BF16),
        scratch_shapes=[pltpu.VMEM((t, d), F32), pltpu.VMEM((t, 1), F32)],
        compiler_params=_params("parallel", "parallel", "arbitrary"),
    )(q, k, v)


def _moba_kernel(q_ref, k_ref, v_ref, bown_ref, bprev_ref, bfar_ref, o_ref,
                 kmean_sc, qaug_sc, m_sc, l_sc, acc_sc, la_sc, lb_sc, *, nb, sub, far):
    blk = MOBA_BLOCK
    d = MOBA_HEAD_DIM
    i = pl.program_id(2)

    @pl.when(i == 0)
    def _():
        kf = k_ref[:, pl.ds(0, d)].astype(F32).reshape(nb, blk, d)
        kmean_sc[...] = jnp.sum(kf, axis=1) * (1.0 / blk)

    q = q_ref[...]
    row_tiles = [pl.ds(r0, sub) for r0 in range(0, blk, sub)]

    off_own = pl.multiple_of(i * blk, blk)
    k_own = k_ref[pl.ds(off_own, blk), pl.ds(0, d)]
    v_own = v_ref[pl.ds(off_own, blk), :]
    own_logits = [_dot_nt(q_ref[rows, :], k_own[:rows.start + sub]) for rows in row_tiles]

    k1, k2, k3 = _split3(kmean_sc[...])
    gate = _dot_nt(q, k1) + _dot_nt(q, k2) + _dot_nt(q, k3)
    lane = lax.broadcasted_iota(jnp.int32, (blk, nb), 1)
    gate = jnp.where(lane < i, gate, -jnp.inf)
    sel = jnp.zeros((blk, nb), F32)
    for _ in range(MOBA_TOPK):
        mx = jnp.max(gate, axis=-1, keepdims=True)
        is_max = jnp.logical_and(gate == mx, mx > -jnp.inf)
        idx = jnp.min(jnp.where(is_max, lane, nb), axis=-1, keepdims=True)
        pick = lane == idx
        sel = jnp.where(pick, 1.0, sel)
        gate = jnp.where(pick, -jnp.inf, gate)
    qaug_sc[:, pl.ds(0, d)] = q
    qaug_sc[:, pl.ds(d, nb)] = (1.0 - sel).astype(BF16)

    for rows, s in zip(row_tiles, own_logits):
        width = rows.start + sub
        r = lax.broadcasted_iota(jnp.int32, (sub, width), 0) + rows.start
        cc = lax.broadcasted_iota(jnp.int32, (sub, width), 1)
        s = jnp.where(cc <= r, s + bown_ref[rows, pl.ds(0, width)], -jnp.inf)
        m0 = jnp.max(s, axis=-1, keepdims=True)
        p = jnp.exp(s - m0)
        m_sc[rows, :] = m0
        l_sc[rows, :] = jnp.sum(p, axis=-1, keepdims=True)
        acc_sc[rows, :] = _dot(p.astype(BF16), v_own[:width])

    def update(logits, v, shift=None):
        m_all, l_all, acc_all = m_sc[...], l_sc[...], acc_sc[...]
        m_out, l_out, acc_out = [], [], []
        for rows, s in zip(row_tiles, logits):
            lo, hi = rows.start, rows.start + sub
            m_prev = m_all[lo:hi]
            m_cur = jnp.max(s, axis=-1, keepdims=True)
            if shift is None:
                m_new = jnp.maximum(m_prev, m_cur)
                p = jnp.exp(s - m_new)
            else:
                m_new = jnp.maximum(m_prev, m_cur + shift)
                p = jnp.exp(s - (m_new - shift))
            alpha = jnp.exp(m_prev - m_new)
            l_out.append(alpha * l_all[lo:hi] + jnp.sum(p, axis=-1, keepdims=True))
            acc_out.append(alpha * acc_all[lo:hi] + _dot(p.astype(BF16), v))
            m_out.append(m_new)
        m_sc[...] = jnp.concatenate(m_out, axis=0)
        l_sc[...] = jnp.concatenate(l_out, axis=0)
        acc_sc[...] = jnp.concatenate(acc_out, axis=0)

    def far_logits(g, buf):
        off = pl.multiple_of(g * (far * blk), far * blk)
        k = k_ref[pl.ds(off, far * blk), :]
        for rows in row_tiles:
            buf[rows, :] = _dot_nt(qaug_sc[rows, :], k)

    def far_consume(g, buf):
        off = pl.multiple_of(g * (far * blk), far * blk)
        update([buf[rows, :] for rows in row_tiles], v_ref[pl.ds(off, far * blk), :],
               shift=bfar_ref[:, pl.ds(0, 1)])

    n_groups = (jnp.maximum(i, 1) - 1) // far
    last = n_groups - 1

    @pl.when(n_groups > 0)
    def _():
        far_logits(0, la_sc)

    def far_pair(pair, carry):
        g = 2 * pair
        far_logits(g + 1, lb_sc)
        far_consume(g, la_sc)
        far_logits(g + 2, la_sc)
        far_consume(g + 1, lb_sc)
        return carry

    lax.fori_loop(0, lax.shift_right_logical(jnp.maximum(last, 0), 1), far_pair, 0)

    @pl.when(jnp.logical_and(n_groups > 0, (last & 1) == 0))
    def _():
        far_consume(last, la_sc)

    @pl.when(jnp.logical_and(n_groups > 0, (last & 1) == 1))
    def _():
        far_logits(last, lb_sc)
        far_consume(last - 1, la_sc)
        far_consume(last, lb_sc)

    tail = i - n_groups * far
    for w in range(1, far + 1):
        @pl.when(tail == w)
        def _(w=w):
            off = pl.multiple_of((i - w) * blk, blk)
            k = k_ref[pl.ds(off, w * blk), :]
            v = v_ref[pl.ds(off, w * blk), :]
            logits = []
            for rows in row_tiles:
                s = _dot_nt(qaug_sc[rows, :], k)
                cut = (w - 1) * blk
                prev = s[:, cut:] + bprev_ref[rows, :]
                logits.append(prev if w == 1 else
                              jnp.concatenate([s[:, :cut] + bfar_ref[:, pl.ds(0, cut)], prev], axis=1))
            update(logits, v)

    o_ref[...] = (acc_sc[...] / l_sc[...]).astype(o_ref.dtype)


def _t5_bucket(dist):
    n = jnp.maximum(dist, 0)
    nf = jnp.maximum(n, 1).astype(F32)
    large = REL_MAX_EXACT + (jnp.log(nf / REL_MAX_EXACT) / math.log(REL_MAX_DIST / REL_MAX_EXACT)
                             * (REL_BUCKETS - REL_MAX_EXACT)).astype(jnp.int32)
    large = jnp.minimum(large, REL_BUCKETS - 1)
    return jnp.where(n < REL_MAX_EXACT, n, large)


def _moba_marks(seq):
    nb = seq // MOBA_BLOCK
    own_block = (jnp.arange(seq)[:, None] // MOBA_BLOCK) == jnp.arange(nb)[None, :]
    return jnp.where(own_block, -MOBA_MASK, 0.0).astype(BF16)


def _moba_attention(q, k_aug, v, rel_bias):
    b, h, s, d = q.shape
    blk = MOBA_BLOCK
    assert s % blk == 0 and d == MOBA_HEAD_DIM
    assert 2 * blk > REL_MAX_DIST
    nb = s // blk
    far = min(MOBA_FAR, nb)
    offs = jnp.arange(blk)
    d_own = offs[:, None] - offs[None, :]
    b_own = rel_bias[:, _t5_bucket(d_own)]
    b_prev = rel_bias[:, _t5_bucket(d_own + blk)]
    b_far = jnp.broadcast_to(rel_bias[:, REL_BUCKETS - 1][:, None, None], (h, 1, far * blk))
    return pl.pallas_call(
        functools.partial(_moba_kernel, nb=nb, sub=MOBA_SUB, far=far),
        grid=(b, h, nb),
        in_specs=[pl.BlockSpec((None, None, blk, d), lambda bi, hi, i: (bi, hi, i, 0)),
                  pl.BlockSpec((None, None, s, d + nb), lambda bi, hi, i: (bi, hi, 0, 0)),
                  pl.BlockSpec((None, None, s, d), lambda bi, hi, i: (bi, hi, 0, 0)),
                  pl.BlockSpec((None, blk, blk), lambda bi, hi, i: (hi, 0, 0)),
                  pl.BlockSpec((None, blk, blk), lambda bi, hi, i: (hi, 0, 0)),
                  pl.BlockSpec((None, 1, far * blk), lambda bi, hi, i: (hi, 0, 0))],
        out_specs=pl.BlockSpec((None, None, blk, d), lambda bi, hi, i: (bi, hi, i, 0)),
        out_shape=jax.ShapeDtypeStruct((b, h, s, d), BF16),
        scratch_shapes=[pltpu.VMEM((nb, d), F32), pltpu.VMEM((blk, d + nb), BF16), pltpu.VMEM((blk, 1), F32),
                        pltpu.VMEM((blk, 1), F32), pltpu.VMEM((blk, d), F32),
                        pltpu.VMEM((blk, far * blk), F32), pltpu.VMEM((blk, far * blk), F32)],
        compiler_params=_params("parallel", "parallel", "arbitrary"),
    )(q, k_aug, v, b_own, b_prev, b_far)


def _l1_proj_kernel(x_ref, g_ref, w_ref, marks_ref, q_ref, k_ref, v_ref):
    d = MOBA_HEAD_DIM
    width = MOBA_HEADS * d
    hn = _rms(x_ref[...], g_ref[...]).astype(BF16)
    y = _dot(hn, w_ref[...])
    marks = marks_ref[...]
    for h in range(MOBA_HEADS):
        q_ref[h] = (_heads(y[:, :width], h, d) * (d ** -0.5)).astype(BF16)
        k_ref[h] = jnp.concatenate([_heads(y[:, width:2 * width], h, d).astype(BF16), marks], axis=-1)
        v_ref[h] = _heads(y[:, 2 * width:], h, d).astype(BF16)


def _l1_proj(x, g, w_qkv, marks):
    b, seq, dm = x.shape
    tm = min(ROW_TILE, seq)
    nb = marks.shape[1]
    d = MOBA_HEAD_DIM
    heads = lambda n: pl.BlockSpec((None, MOBA_HEADS, tm, n), lambda bi, i: (bi, 0, i, 0))
    return pl.pallas_call(
        _l1_proj_kernel,
        grid=(b, seq // tm),
        in_specs=[pl.BlockSpec((None, tm, dm), lambda bi, i: (bi, i, 0)), _full((1, dm)), _full(w_qkv.shape),
                  pl.BlockSpec((tm, nb), lambda bi, i: (i, 0))],
        out_specs=[heads(d), heads(d + nb), heads(d)],
        out_shape=[jax.ShapeDtypeStruct((b, MOBA_HEADS, seq, n), BF16) for n in (d, d + nb, d)],
        compiler_params=_params("parallel", "parallel"),
    )(x, g.reshape(1, dm), w_qkv.astype(BF16), marks)


def _out_proj_kernel(*refs):
    *a_refs, w_ref, x_ref, o_ref = refs
    merged = jnp.concatenate([a_ref[h] for a_ref in a_refs for h in range(a_ref.shape[0])], axis=-1)
    o_ref[...] = x_ref[...] + _dot(merged, w_ref[...])


def _out_proj(mixed, w, x):
    b, seq, dm = x.shape
    tm = min(ROW_TILE, seq)
    row = pl.BlockSpec((None, tm, dm), lambda bi, i: (bi, i, 0))
    return pl.pallas_call(
        _out_proj_kernel,
        grid=(b, seq // tm),
        in_specs=[pl.BlockSpec((None, a.shape[1], tm, a.shape[3]), lambda bi, i: (bi, 0, i, 0)) for a in mixed]
        + [_full(w.shape), row],
        out_specs=row,
        out_shape=jax.ShapeDtypeStruct((b, seq, dm), F32),
        compiler_params=_params("parallel", "parallel"),
    )(*mixed, w.astype(BF16), x)


def _silu(x):
    return x / (1.0 + jnp.exp(-x))


def _swiglu_kernel(x_ref, g_ref, wg_ref, wu_ref, wd_ref, o_ref, *, chunk):
    x = x_ref[...]
    hn = _rms(x, g_ref[...]).astype(BF16)
    o_ref[...] = x
    for c0 in range(0, wg_ref.shape[1], chunk):
        a = _silu(_dot(hn, wg_ref[:, c0:c0 + chunk])) * _dot(hn, wu_ref[:, c0:c0 + chunk])
        o_ref[...] += _dot(a.astype(BF16), wd_ref[c0:c0 + chunk, :])


def _swiglu(x2, g, w_gate, w_up, w_down):
    rows, dm = x2.shape
    tm = min(ROW_TILE, rows)
    return pl.pallas_call(
        functools.partial(_swiglu_kernel, chunk=512),
        grid=(rows // tm,),
        in_specs=[pl.BlockSpec((tm, dm), lambda i: (i, 0)), _full((1, dm)), _full(w_gate.shape),
                  _full(w_up.shape), _full(w_down.shape)],
        out_specs=pl.BlockSpec((tm, dm), lambda i: (i, 0)),
        out_shape=jax.ShapeDtypeStruct((rows, dm), F32),
        compiler_params=_params("parallel"),
    )(x2, g.reshape(1, dm), w_gate.astype(BF16), w_up.astype(BF16), w_down.astype(BF16))


def _moe_kernel(x_ref, g_ref, wr_ref, wg_ref, wu_ref, wd_ref, gf_ref, o_ref, hn_sc, gate_sc, acc_sc):
    e = pl.program_id(1)
    lane = lax.broadcasted_iota(jnp.int32, gate_sc.shape, 1)

    @pl.when(e == 0)
    def _():
        h = _rms(x_ref[...], g_ref[...])
        hn_sc[...] = h.astype(BF16)
        logits = jnp.dot(h, wr_ref[...], preferred_element_type=F32, precision=lax.Precision.HIGHEST)
        logits = jnp.where(lane < N_EXPERTS, logits, -jnp.inf)
        v1 = jnp.max(logits, axis=-1, keepdims=True)
        i1 = jnp.min(jnp.where(logits == v1, lane, LANES), axis=-1, keepdims=True)
        rest = jnp.where(lane == i1, -jnp.inf, logits)
        v2 = jnp.max(rest, axis=-1, keepdims=True)
        i2 = jnp.min(jnp.where(rest == v2, lane, LANES), axis=-1, keepdims=True)
        p2 = jnp.exp(v2 - v1)
        w1 = 1.0 / (1.0 + p2)
        gate_sc[...] = jnp.where(lane == i1, w1, 0.0) + jnp.where(lane == i2, p2 * w1, 0.0)
        acc_sc[...] = jnp.zeros_like(acc_sc)

    hn = hn_sc[...]
    gate = jnp.sum(jnp.where(lane == e, gate_sc[...], 0.0), axis=-1, keepdims=True)
    a = _silu(_dot(hn, wg_ref[...])) * _dot(hn, wu_ref[...]) * gate
    acc_sc[...] += _dot(a.astype(BF16), wd_ref[...])

    @pl.when(e == pl.num_programs(1) - 1)
    def _():
        o_ref[...] = _rms(x_ref[...] + acc_sc[...], gf_ref[...])


def _moe_final(x2, g, w_router, w_gate, w_up, w_down, g_final):
    rows, dm = x2.shape
    tm = min(MOE_ROW_TILE, rows)
    ne, _, dff = w_gate.shape
    w_r = jnp.zeros((dm, LANES), F32).at[:, :ne].set(w_router)
    return pl.pallas_call(
        _moe_kernel,
        grid=(rows // tm, ne),
        in_specs=[pl.BlockSpec((tm, dm), lambda i, e: (i, 0)),
                  pl.BlockSpec((1, dm), lambda i, e: (0, 0)),
                  pl.BlockSpec((dm, LANES), lambda i, e: (0, 0)),
                  pl.BlockSpec((None, dm, dff), lambda i, e: (e, 0, 0)),
                  pl.BlockSpec((None, dm, dff), lambda i, e: (e, 0, 0)),
                  pl.BlockSpec((None, dff, dm), lambda i, e: (e, 0, 0)),
                  pl.BlockSpec((1, dm), lambda i, e: (0, 0))],
        out_specs=pl.BlockSpec((tm, dm), lambda i, e: (i, 0)),
        out_shape=jax.ShapeDtypeStruct((rows, dm), F32),
        scratch_shapes=[pltpu.VMEM((tm, dm), BF16), pltpu.VMEM((tm, LANES), F32), pltpu.VMEM((tm, dm), F32)],
        compiler_params=_params("parallel", "arbitrary"),
    )(x2, g.reshape(1, dm), w_r, w_gate.astype(BF16), w_up.astype(BF16), w_down.astype(BF16),
      g_final.reshape(1, dm))


def _rope_tables(seq):
    half = MLA_ROPE // 2
    inv = 1.0 / (ROPE_THETA ** (jnp.arange(0, MLA_ROPE, 2, dtype=F32) / MLA_ROPE))
    ang = jnp.arange(seq, dtype=F32)[:, None] * inv[None, :]
    reps = LANES // half
    return jnp.tile(jnp.cos(ang), (1, reps)), jnp.tile(jnp.sin(ang), (1, reps))


def kernel(x, ev_attn_norm, ev_w_in, ev_q_norm, ev_w_uq, ev_kv_norm, ev_w_ukv, ev_w_o, ev_ffn_norm, ev_w_gate, ev_w_up, ev_w_down, od_attn_norm, od_w_qkv, od_w_o, od_ffn_norm, od_w_router, od_w_gate, od_w_up, od_w_down, rel_bias, final_norm):
    b, s, dm = x.shape
    cos_t, sin_t = _rope_tables(s)

    q_a, k_a, v_a, q_b, k_b, v_b = _l0_proj(x, ev_attn_norm[0], ev_w_in[0], ev_q_norm[0], ev_w_uq[0],
                                            ev_kv_norm[0], ev_w_ukv[0], cos_t, sin_t)
    o_a = _mla_attention(q_a, k_a, v_a)
    o_b = _sb_attention(q_b, k_b, v_b)
    x = _out_proj([o_a, o_b], ev_w_o[0], x)
    x = _swiglu(x.reshape(b * s, dm), ev_ffn_norm[0], ev_w_gate[0], ev_w_up[0], ev_w_down[0]).reshape(b, s, dm)

    q, k_aug, v = _l1_proj(x, od_attn_norm[0], od_w_qkv[0], _moba_marks(s))
    o_c = _moba_attention(q, k_aug, v, rel_bias)
    x = _out_proj([o_c], od_w_o[0], x)
    out = _moe_final(x.reshape(b * s, dm), od_ffn_norm[0], od_w_router[0], od_w_gate[0], od_w_up[0],
                     od_w_down[0], final_norm)
    return out.reshape(b, s, dm)
```

```python
import functools
import math

import jax
import jax.numpy as jnp
import numpy as np
from jax import lax
from jax.experimental import pallas as pl
from jax.experimental.pallas import tpu as pltpu

MLA_HEADS = 8
MLA_NOPE = 64
MLA_ROPE = 32
MLA_V = 64
Q_LORA = 384
KV_LORA = 256
ROPE_THETA = 10000.0
SB_HEADS = 8
SB_HEAD_DIM = 64
MOBA_HEADS = 16
MOBA_HEAD_DIM = 64
MOBA_BLOCK = 256
MOBA_TOPK = 3
REL_BUCKETS = 32
REL_MAX_EXACT = REL_BUCKETS // 2
REL_MAX_DIST = 128
N_EXPERTS = 8
TOP_K = 2
RMS_EPS = 1e-6

LANES = 128
VMEM_LIMIT = 56 * 1024 * 1024
ROW_TILE = 512
MOE_ROW_TILE = 1024
MLA_TILE = 2048
MLA_SUB = 256
MOBA_FAR = 8
MOBA_SELECT_TILES = 4
MOBA_SUB = 128
MOBA_MASK = 2.0 ** 100
SB_TILE = 256
SB_UNDERFLOW = -128.0
LOG2E = 1.4426950408889634

BF16 = jnp.bfloat16
F32 = jnp.float32


def _params(*sem):
    return pltpu.CompilerParams(dimension_semantics=sem, vmem_limit_bytes=VMEM_LIMIT)


def _rms(x, g):
    return x * lax.rsqrt(jnp.mean(x * x, axis=-1, keepdims=True) + RMS_EPS) * g


def _dot(a, b):
    return jnp.dot(a, b, preferred_element_type=F32)


def _dot_nt(a, b):
    return lax.dot_general(a, b, (((1,), (1,)), ((), ())), preferred_element_type=F32)


def _full(shape):
    return pl.BlockSpec(shape, lambda *_: (0,) * len(shape))


def _heads(y, h, d):
    return y[:, h * d:(h + 1) * d]


def _l0_proj_kernel(x_ref, g_ref, wcq_ref, wckv_ref, wkr_ref, wsb_ref, qn_ref, wuqn_ref, wuqr_ref,
                    kvn_ref, wukk_ref, wukv_ref, cos_ref, sin_ref,
                    qa_ref, ka_ref, va_ref, sbq_ref, sbk_ref, sbv_ref):
    hn = _rms(x_ref[...], g_ref[...]).astype(BF16)
    c = cos_ref[...]
    s = sin_ref[...]
    half = MLA_ROPE // 2

    def rope(t):
        t1 = t[:, :LANES]
        t2 = t[:, LANES:]
        return t1 * c - t2 * s, t2 * c + t1 * s

    cq = _rms(_dot(hn, wcq_ref[...]), qn_ref[...]).astype(BF16)
    qn = _dot(cq, wuqn_ref[...])
    r1, r2 = rope(_dot(cq, wuqr_ref[...]))
    ckv = _rms(_dot(hn, wckv_ref[...]), kvn_ref[...]).astype(BF16)
    kn = _dot(ckv, wukk_ref[...])
    va = _dot(ckv, wukv_ref[...])
    k1, k2 = rope(_dot(hn, wkr_ref[...]))
    k_rope = jnp.concatenate([k1[:, :half], k2[:, :half]], axis=-1).astype(BF16)
    for h in range(MLA_HEADS):
        qa_ref[h] = jnp.concatenate([_heads(qn, h, MLA_NOPE), _heads(r1, h, half), _heads(r2, h, half)],
                                    axis=-1).astype(BF16)
        ka_ref[h] = jnp.concatenate([_heads(kn, h, MLA_NOPE).astype(BF16), k_rope], axis=-1)
        va_ref[h] = _heads(va, h, MLA_V).astype(BF16)

    d = SB_HEADS * SB_HEAD_DIM
    sb = _dot(hn, wsb_ref[...])
    for h in range(SB_HEADS):
        sbq_ref[h] = (_heads(sb[:, :d], h, SB_HEAD_DIM) * (SB_HEAD_DIM ** -0.5)).astype(BF16)
        sbk_ref[h] = _heads(sb[:, d:2 * d], h, SB_HEAD_DIM).astype(BF16)
        sbv_ref[h] = _heads(sb[:, 2 * d:], h, SB_HEAD_DIM).astype(BF16)


def _l0_proj(x, g, w_in, q_norm, w_uq, kv_norm, w_ukv, cos_t, sin_t):
    b, seq, dm = x.shape
    tm = min(ROW_TILE, seq)
    half = MLA_ROPE // 2
    o_kr = Q_LORA + KV_LORA
    o_sb = o_kr + MLA_ROPE
    zpad = jnp.zeros((dm, LANES - half), F32)
    w_cq = w_in[:, :Q_LORA].astype(BF16)
    w_ckv = w_in[:, Q_LORA:o_kr].astype(BF16)
    w_kr = jnp.concatenate([w_in[:, o_kr:o_kr + half], zpad, w_in[:, o_kr + half:o_sb], zpad],
                           axis=1).astype(BF16)
    w_sb = w_in[:, o_sb:].astype(BF16)
    uq = w_uq.reshape(Q_LORA, MLA_HEADS, MLA_NOPE + MLA_ROPE)
    w_uqn = uq[:, :, :MLA_NOPE].reshape(Q_LORA, -1).astype(BF16)
    w_uqr = jnp.concatenate([uq[:, :, MLA_NOPE:MLA_NOPE + half].reshape(Q_LORA, -1),
                             uq[:, :, MLA_NOPE + half:].reshape(Q_LORA, -1)], axis=1).astype(BF16)
    ukv = w_ukv.reshape(KV_LORA, MLA_HEADS, MLA_NOPE + MLA_V)
    w_ukk = ukv[:, :, :MLA_NOPE].reshape(KV_LORA, -1).astype(BF16)
    w_ukv2 = ukv[:, :, MLA_NOPE:].reshape(KV_LORA, -1).astype(BF16)
    tab = pl.BlockSpec((tm, LANES), lambda bi, i: (i, 0))
    heads = lambda n, d: pl.BlockSpec((None, n, tm, d), lambda bi, i: (bi, 0, i, 0))
    out_dims = [(MLA_HEADS, MLA_NOPE + MLA_ROPE), (MLA_HEADS, MLA_NOPE + MLA_ROPE), (MLA_HEADS, MLA_V),
                (SB_HEADS, SB_HEAD_DIM), (SB_HEADS, SB_HEAD_DIM), (SB_HEADS, SB_HEAD_DIM)]
    return pl.pallas_call(
        _l0_proj_kernel,
        grid=(b, seq // tm),
        in_specs=[pl.BlockSpec((None, tm, dm), lambda bi, i: (bi, i, 0)), _full((1, dm)), _full(w_cq.shape),
                  _full(w_ckv.shape), _full(w_kr.shape), _full(w_sb.shape), _full((1, Q_LORA)),
                  _full(w_uqn.shape), _full(w_uqr.shape), _full((1, KV_LORA)), _full(w_ukk.shape),
                  _full(w_ukv2.shape), tab, tab],
        out_specs=[heads(n, d) for n, d in out_dims],
        out_shape=[jax.ShapeDtypeStruct((b, n, seq, d), BF16) for n, d in out_dims],
        compiler_params=_params("parallel", "parallel"),
    )(x, g.reshape(1, dm), w_cq, w_ckv, w_kr, w_sb, q_norm.reshape(1, -1), w_uqn, w_uqr,
      kv_norm.reshape(1, -1), w_ukk, w_ukv2, cos_t, sin_t)


def _mla_kernel(q_ref, k_ref, v_ref, o_ref, m_sc, l_sc, acc_sc, *, t, sub, c):
    qi = pl.program_id(2)
    m_sc[...] = jnp.full_like(m_sc, -jnp.inf)
    l_sc[...] = jnp.zeros_like(l_sc)
    acc_sc[...] = jnp.zeros_like(acc_sc)

    def tile(j, diagonal):
        off = pl.multiple_of(j * t, t)
        k = k_ref[pl.ds(off, t), :]
        v = v_ref[pl.ds(off, t), :]
        m_all, l_all, acc_all = m_sc[...], l_sc[...], acc_sc[...]
        m_out, l_out, acc_out = [], [], []
        widths = [r0 + sub if diagonal else t for r0 in range(0, t, sub)]
        logits = [_dot_nt(q_ref[pl.ds(r0, sub), :], k[:w]) for r0, w in zip(range(0, t, sub), widths)]
        for r0, width, s in zip(range(0, t, sub), widths, logits):
            if diagonal:
                r = lax.broadcasted_iota(jnp.int32, (sub, width), 0) + r0
                cc = lax.broadcasted_iota(jnp.int32, (sub, width), 1)
                s = jnp.where(cc <= r, s, -jnp.inf)
            m_prev = m_all[r0:r0 + sub]
            m_new = jnp.maximum(m_prev, jnp.max(s, axis=-1, keepdims=True))
            alpha = jnp.exp2((m_prev - m_new) * c)
            p = jnp.exp2((s - m_new) * c)
            l_out.append(alpha * l_all[r0:r0 + sub] + jnp.sum(p, axis=-1, keepdims=True))
            acc_out.append(alpha * acc_all[r0:r0 + sub] + _dot(p.astype(BF16), v[:width]))
            m_out.append(m_new)
        m_sc[...] = jnp.concatenate(m_out, axis=0)
        l_sc[...] = jnp.concatenate(l_out, axis=0)
        acc_sc[...] = jnp.concatenate(acc_out, axis=0)

    def body(j, carry):
        tile(j, False)
        return carry

    lax.fori_loop(0, qi, body, 0)
    tile(qi, True)
    o_ref[...] = (acc_sc[...] / l_sc[...]).astype(o_ref.dtype)


def _mla_attention(q, k, v):
    b, h, s, dq = q.shape
    dv = v.shape[-1]
    t = min(MLA_TILE, s)
    c = (MLA_NOPE + MLA_ROPE) ** -0.5 * LOG2E
    return pl.pallas_call(
        functools.partial(_mla_kernel, t=t, sub=min(MLA_SUB, t), c=c),
        grid=(b, h, s // t),
        in_specs=[pl.BlockSpec((None, None, t, dq), lambda bi, hi, i: (bi, hi, i, 0)),
                  pl.BlockSpec((None, None, s, dq), lambda bi, hi, i: (bi, hi, 0, 0)),
                  pl.BlockSpec((None, None, s, dv), lambda bi, hi, i: (bi, hi, 0, 0))],
        out_specs=pl.BlockSpec((None, None, t, dv), lambda bi, hi, i: (bi, hi, i, 0)),
        out_shape=jax.ShapeDtypeStruct((b, h, s, dv), BF16),
        scratch_shapes=[pltpu.VMEM((t, 1), F32), pltpu.VMEM((t, 1), F32), pltpu.VMEM((t, dv), F32)],
        compiler_params=_params("parallel", "parallel", "arbitrary"),
    )(q, k, v)


def _split3(x):
    hi = x.astype(BF16)
    r = x - hi.astype(F32)
    mid = r.astype(BF16)
    lo = (r - mid.astype(F32)).astype(BF16)
    return hi, mid, lo


def _sb_kernel(q_ref, k_ref, v_ref, o_ref, acc_sc, run_sc, *, t):
    qi = pl.program_id(2)
    q = q_ref[...]
    acc_sc[...] = jnp.zeros_like(acc_sc)
    run_sc[...] = jnp.zeros_like(run_sc)
    r = lax.broadcasted_iota(jnp.int32, (t, t), 0)
    cc = lax.broadcasted_iota(jnp.int32, (t, t), 1)
    later = (r > cc).astype(BF16)

    past = cc < r

    def scores(j):
        return _dot_nt(q, k_ref[pl.ds(pl.multiple_of(j * t, t), t), :])

    def log_terms(z, diagonal):
        log_fail = jnp.minimum(-z, 0.0) - jnp.log(1.0 + jnp.exp(-jnp.abs(z)))
        log_hit = log_fail + z
        if diagonal:
            log_fail = jnp.where(past, log_fail, 0.0)
        hi, mid, lo = _split3(log_fail)
        within = _dot(hi, later) + _dot(mid, later) + _dot(lo, later)
        return log_hit + within, jnp.sum(log_fail, axis=-1, keepdims=True)

    def weighted(j, log_w, run, diagonal):
        w = jnp.exp(log_w + run)
        if diagonal:
            w = jnp.where(past, w, 0.0)
        return _dot(w.astype(BF16), v_ref[pl.ds(pl.multiple_of(j * t, t), t), :])

    def tile(j, diagonal):
        log_w, total = log_terms(scores(j), diagonal)
        acc_sc[...] += weighted(j, log_w, run_sc[...], diagonal)
        run_sc[...] += total

    @pl.when(qi == 0)
    def _():
        tile(qi, True)

    @pl.when(qi > 0)
    def _():
        z_diag, z_prev = scores(qi), scores(qi - 1)
        lw_diag, total_diag = log_terms(z_diag, True)
        lw_prev, total_prev = log_terms(z_prev, False)
        acc_sc[...] = weighted(qi, lw_diag, 0.0, True) + weighted(qi - 1, lw_prev, total_diag, False)
        run_sc[...] = total_diag + total_prev

    def cond(carry):
        j, worst = carry
        return jnp.logical_and(j >= 0, worst > SB_UNDERFLOW)

    def body(carry):
        j, _ = carry
        tile(j, False)
        return j - 1, jnp.max(run_sc[...])

    lax.while_loop(cond, body, (qi - 2, jnp.max(run_sc[...])))
    o_ref[...] = acc_sc[...].astype(o_ref.dtype)


def _sb_attention(q, k, v):
    b, h, s, d = q.shape
    t = min(SB_TILE, s)
    return pl.pallas_call(
        functools.partial(_sb_kernel, t=t),
        grid=(b, h, s // t),
        in_specs=[pl.BlockSpec((None, None, t, d), lambda bi, hi, i: (bi, hi, i, 0)),
                  pl.BlockSpec((None, None, s, d), lambda bi, hi, i: (bi, hi, 0, 0)),
                  pl.BlockSpec((None, None, s, d), lambda bi, hi, i: (bi, hi, 0, 0))],
        out_specs=pl.BlockSpec((None, None, t, d), lambda bi, hi, i: (bi, hi, i, 0)),
        out_shape=jax.ShapeDtypeStruct((b, h, s, d), BF16),
        scratch_shapes=[pltpu.VMEM((t, d), F32), pltpu.VMEM((t, 1), F32)],
        compiler_params=_params("parallel", "parallel", "arbitrary"),
    )(q, k, v)


def _moba_select_kernel(q_ref, k_ref, qaug_ref, kmean_sc, *, nb, tiles):
    blk = MOBA_BLOCK
    d = MOBA_HEAD_DIM
    step = pl.program_id(2)

    @pl.when(step == 0)
    def _():
        kf = k_ref[:, pl.ds(0, d)].astype(F32).reshape(nb, blk, d)
        kmean_sc[...] = jnp.sum(kf, axis=1) * (1.0 / blk)

    k1, k2, k3 = _split3(kmean_sc[...])
    row = lax.broadcasted_iota(jnp.int32, (nb, blk), 0)
    eye = (lax.broadcasted_iota(jnp.int32, (nb, nb), 0) == lax.broadcasted_iota(jnp.int32, (nb, nb), 1))
    for t in range(tiles):
        i = step * tiles + t
        rows = pl.ds(t * blk, blk)
        q = q_ref[rows, :]
        gate = _dot_nt(k1, q) + _dot_nt(k2, q) + _dot_nt(k3, q)
        gate = jnp.where(row < i, gate, -jnp.inf)
        masked = jnp.where(row == i, 0.0, 1.0)
        for _ in range(MOBA_TOPK):
            mx = jnp.max(gate, axis=0, keepdims=True)
            is_max = jnp.logical_and(gate == mx, mx > -jnp.inf)
            idx = jnp.min(jnp.where(is_max, row, nb), axis=0, keepdims=True)
            pick = row == idx
            masked = jnp.where(pick, 0.0, masked)
            gate = jnp.where(pick, -jnp.inf, gate)
        masked_t = lax.dot_general(masked.astype(BF16), eye.astype(BF16), (((0,), (0,)), ((), ())),
                                   preferred_element_type=F32)
        qaug_ref[rows, pl.ds(0, d)] = q
        qaug_ref[rows, pl.ds(d, nb)] = masked_t.astype(BF16)


def _moba_select(q, k_aug):
    b, h, s, d = q.shape
    nb = s // MOBA_BLOCK
    tiles = min(MOBA_SELECT_TILES, nb)
    t = tiles * MOBA_BLOCK
    return pl.pallas_call(
        functools.partial(_moba_select_kernel, nb=nb, tiles=tiles),
        grid=(b, h, s // t),
        in_specs=[pl.BlockSpec((None, None, t, d), lambda bi, hi, i: (bi, hi, i, 0)),
                  pl.BlockSpec((None, None, s, d + nb), lambda bi, hi, i: (bi, hi, 0, 0))],
        out_specs=pl.BlockSpec((None, None, t, d + nb), lambda bi, hi, i: (bi, hi, i, 0)),
        out_shape=jax.ShapeDtypeStruct((b, h, s, d + nb), BF16),
        scratch_shapes=[pltpu.VMEM((nb, d), F32)],
        compiler_params=_params("parallel", "parallel", "arbitrary"),
    )(q, k_aug)


def _moba_kernel(qaug_sc, k_ref, v_ref, bown_ref, bprev_ref, bfar_ref, o_ref,
                 m_sc, l_sc, acc_sc, la_sc, lb_sc, *, sub, far):
    blk = MOBA_BLOCK
    i = pl.program_id(2)
    row_tiles = [pl.ds(r0, sub) for r0 in range(0, blk, sub)]
    n_groups = (jnp.maximum(i, 1) - 1) // far

    tail = i - n_groups * far
    for w in range(far + 1):
        @pl.when(tail == w)
        def _(w=w):
            off = pl.multiple_of((i - w) * blk, blk)
            k = k_ref[pl.ds(off, (w + 1) * blk), :]
            v = v_ref[pl.ds(off, (w + 1) * blk), :]
            logits = [_dot_nt(qaug_sc[rows, :], k) for rows in row_tiles]
            for rows, s in zip(row_tiles, logits):
                r = lax.broadcasted_iota(jnp.int32, (sub, blk), 0) + rows.start
                cc = lax.broadcasted_iota(jnp.int32, (sub, blk), 1)
                parts = [jnp.where(cc <= r, s[:, w * blk:] + bown_ref[rows, :], -jnp.inf)]
                if w >= 1:
                    parts.insert(0, s[:, (w - 1) * blk:w * blk] + bprev_ref[rows, :])
                if w >= 2:
                    parts.insert(0, s[:, :(w - 1) * blk] + bfar_ref[:, pl.ds(0, (w - 1) * blk)])
                s = parts[0] if len(parts) == 1 else jnp.concatenate(parts, axis=1)
                m0 = jnp.max(s, axis=-1, keepdims=True)
                p = jnp.exp(s - m0)
                m_sc[rows, :] = m0
                l_sc[rows, :] = jnp.sum(p, axis=-1, keepdims=True)
                acc_sc[rows, :] = _dot(p.astype(BF16), v)

    def update(logits, v, shift=None):
        m_all, l_all, acc_all = m_sc[...], l_sc[...], acc_sc[...]
        m_out, l_out, acc_out = [], [], []
        for rows, s in zip(row_tiles, logits):
            lo, hi = rows.start, rows.start + sub
            m_prev = m_all[lo:hi]
            m_cur = jnp.max(s, axis=-1, keepdims=True)
            if shift is None:
                m_new = jnp.maximum(m_prev, m_cur)
                p = jnp.exp(s - m_new)
            else:
                m_new = jnp.maximum(m_prev, m_cur + shift)
                p = jnp.exp(s - (m_new - shift))
            alpha = jnp.exp(m_prev - m_new)
            l_out.append(alpha * l_all[lo:hi] + jnp.sum(p, axis=-1, keepdims=True))
            acc_out.append(alpha * acc_all[lo:hi] + _dot(p.astype(BF16), v))
            m_out.append(m_new)
        m_sc[...] = jnp.concatenate(m_out, axis=0)
        l_sc[...] = jnp.concatenate(l_out, axis=0)
        acc_sc[...] = jnp.concatenate(acc_out, axis=0)

    def far_logits(g, buf):
        off = pl.multiple_of(g * (far * blk), far * blk)
        k = k_ref[pl.ds(off, far * blk), :]
        for rows in row_tiles:
            buf[rows, :] = _dot_nt(qaug_sc[rows, :], k)

    def far_consume(g, buf):
        off = pl.multiple_of(g * (far * blk), far * blk)
        update([buf[rows, :] for rows in row_tiles], v_ref[pl.ds(off, far * blk), :],
               shift=bfar_ref[:, pl.ds(0, 1)])

    last = n_groups - 1

    @pl.when(n_groups > 0)
    def _():
        far_logits(0, la_sc)

    def far_pair(pair, carry):
        g = 2 * pair
        far_logits(g + 1, lb_sc)
        far_consume(g, la_sc)
        far_logits(g + 2, la_sc)
        far_consume(g + 1, lb_sc)
        return carry

    lax.fori_loop(0, lax.shift_right_logical(jnp.maximum(last, 0), 1), far_pair, 0)

    @pl.when(jnp.logical_and(n_groups > 0, (last & 1) == 0))
    def _():
        far_consume(last, la_sc)

    @pl.when(jnp.logical_and(n_groups > 0, (last & 1) == 1))
    def _():
        far_logits(last, lb_sc)
        far_consume(last - 1, la_sc)
        far_consume(last, lb_sc)

    o_ref[...] = (acc_sc[...] / l_sc[...]).astype(o_ref.dtype)


def _t5_bucket(dist):
    n = jnp.maximum(dist, 0)
    nf = jnp.maximum(n, 1).astype(F32)
    large = REL_MAX_EXACT + (jnp.log(nf / REL_MAX_EXACT) / math.log(REL_MAX_DIST / REL_MAX_EXACT)
                             * (REL_BUCKETS - REL_MAX_EXACT)).astype(jnp.int32)
    large = jnp.minimum(large, REL_BUCKETS - 1)
    return jnp.where(n < REL_MAX_EXACT, n, large)


def _moba_marks(seq):
    nb = seq // MOBA_BLOCK
    own_block = (jnp.arange(seq)[:, None] // MOBA_BLOCK) == jnp.arange(nb)[None, :]
    return jnp.where(own_block, -MOBA_MASK, 0.0).astype(BF16)


def _moba_attention(q_aug, k_aug, v, rel_bias):
    b, h, s, d = v.shape
    blk = MOBA_BLOCK
    assert s % blk == 0 and d == MOBA_HEAD_DIM
    assert 2 * blk > REL_MAX_DIST
    nb = s // blk
    far = min(MOBA_FAR, nb)
    offs = jnp.arange(blk)
    d_own = offs[:, None] - offs[None, :]

    def bias_table(dist):
        onehot = _t5_bucket(dist)[None, :, :] == jnp.arange(REL_BUCKETS)[:, None, None]
        return jnp.sum(jnp.where(onehot[None], rel_bias[:, :, None, None], 0.0), axis=1)

    b_own = bias_table(d_own)
    b_prev = bias_table(d_own + blk)
    b_far = jnp.broadcast_to(rel_bias[:, REL_BUCKETS - 1][:, None, None], (h, 1, far * blk))
    return pl.pallas_call(
        functools.partial(_moba_kernel, sub=MOBA_SUB, far=far),
        grid=(b, h, nb),
        in_specs=[pl.BlockSpec((None, None, blk, d + nb), lambda bi, hi, i: (bi, hi, i, 0)),
                  pl.BlockSpec((None, None, s, d + nb), lambda bi, hi, i: (bi, hi, 0, 0)),
                  pl.BlockSpec((None, None, s, d), lambda bi, hi, i: (bi, hi, 0, 0)),
                  pl.BlockSpec((None, blk, blk), lambda bi, hi, i: (hi, 0, 0)),
                  pl.BlockSpec((None, blk, blk), lambda bi, hi, i: (hi, 0, 0)),
                  pl.BlockSpec((None, 1, far * blk), lambda bi, hi, i: (hi, 0, 0))],
        out_specs=pl.BlockSpec((None, None, blk, d), lambda bi, hi, i: (bi, hi, i, 0)),
        out_shape=jax.ShapeDtypeStruct((b, h, s, d), BF16),
        scratch_shapes=[pltpu.VMEM((blk, 1), F32), pltpu.VMEM((blk, 1), F32), pltpu.VMEM((blk, d), F32),
                        pltpu.VMEM((blk, far * blk), F32), pltpu.VMEM((blk, far * blk), F32)],
        compiler_params=_params("parallel", "parallel", "arbitrary"),
    )(q_aug, k_aug, v, b_own, b_prev, b_far)


def _l1_proj_kernel(x_ref, g_ref, w_ref, marks_ref, q_ref, k_ref, v_ref):
    d = MOBA_HEAD_DIM
    width = MOBA_HEADS * d
    hn = _rms(x_ref[...], g_ref[...]).astype(BF16)
    y = _dot(hn, w_ref[...])
    marks = marks_ref[...]
    for h in range(MOBA_HEADS):
        q_ref[h] = (_heads(y[:, :width], h, d) * (d ** -0.5)).astype(BF16)
        k_ref[h] = jnp.concatenate([_heads(y[:, width:2 * width], h, d).astype(BF16), marks], axis=-1)
        v_ref[h] = _heads(y[:, 2 * width:], h, d).astype(BF16)


def _l1_proj(x, g, w_qkv, marks):
    b, seq, dm = x.shape
    tm = min(ROW_TILE, seq)
    nb = marks.shape[1]
    d = MOBA_HEAD_DIM
    heads = lambda n: pl.BlockSpec((None, MOBA_HEADS, tm, n), lambda bi, i: (bi, 0, i, 0))
    return pl.pallas_call(
        _l1_proj_kernel,
        grid=(b, seq // tm),
        in_specs=[pl.BlockSpec((None, tm, dm), lambda bi, i: (bi, i, 0)), _full((1, dm)), _full(w_qkv.shape),
                  pl.BlockSpec((tm, nb), lambda bi, i: (i, 0))],
        out_specs=[heads(d), heads(d + nb), heads(d)],
        out_shape=[jax.ShapeDtypeStruct((b, MOBA_HEADS, seq, n), BF16) for n in (d, d + nb, d)],
        compiler_params=_params("parallel", "parallel"),
    )(x, g.reshape(1, dm), w_qkv.astype(BF16), marks)


def _out_proj_kernel(*refs):
    *a_refs, w_ref, x_ref, o_ref = refs
    merged = jnp.concatenate([a_ref[h] for a_ref in a_refs for h in range(a_ref.shape[0])], axis=-1)
    o_ref[...] = x_ref[...] + _dot(merged, w_ref[...])


def _out_proj(mixed, w, x):
    b, seq, dm = x.shape
    tm = min(ROW_TILE, seq)
    row = pl.BlockSpec((None, tm, dm), lambda bi, i: (bi, i, 0))
    return pl.pallas_call(
        _out_proj_kernel,
        grid=(b, seq // tm),
        in_specs=[pl.BlockSpec((None, a.shape[1], tm, a.shape[3]), lambda bi, i: (bi, 0, i, 0)) for a in mixed]
        + [_full(w.shape), row],
        out_specs=row,
        out_shape=jax.ShapeDtypeStruct((b, seq, dm), F32),
        compiler_params=_params("parallel", "parallel"),
    )(*mixed, w.astype(BF16), x)


def _silu(x):
    return x / (1.0 + jnp.exp(-x))


def _swiglu_kernel(x_ref, g_ref, wg_ref, wu_ref, wd_ref, o_ref, *, chunk):
    x = x_ref[...]
    hn = _rms(x, g_ref[...]).astype(BF16)
    o_ref[...] = x
    for c0 in range(0, wg_ref.shape[1], chunk):
        a = _silu(_dot(hn, wg_ref[:, c0:c0 + chunk])) * _dot(hn, wu_ref[:, c0:c0 + chunk])
        o_ref[...] += _dot(a.astype(BF16), wd_ref[c0:c0 + chunk, :])


def _swiglu(x2, g, w_gate, w_up, w_down):
    rows, dm = x2.shape
    tm = min(ROW_TILE, rows)
    return pl.pallas_call(
        functools.partial(_swiglu_kernel, chunk=512),
        grid=(rows // tm,),
        in_specs=[pl.BlockSpec((tm, dm), lambda i: (i, 0)), _full((1, dm)), _full(w_gate.shape),
                  _full(w_up.shape), _full(w_down.shape)],
        out_specs=pl.BlockSpec((tm, dm), lambda i: (i, 0)),
        out_shape=jax.ShapeDtypeStruct((rows, dm), F32),
        compiler_params=_params("parallel"),
    )(x2, g.reshape(1, dm), w_gate.astype(BF16), w_up.astype(BF16), w_down.astype(BF16))


def _moe_kernel(x_ref, g_ref, wr_ref, wg_ref, wu_ref, wd_ref, gf_ref, o_ref, hn_sc, gate_sc, acc_sc):
    e = pl.program_id(1)
    lane = lax.broadcasted_iota(jnp.int32, gate_sc.shape, 1)

    @pl.when(e == 0)
    def _():
        h = _rms(x_ref[...], g_ref[...])
        hn_sc[...] = h.astype(BF16)
        logits = jnp.dot(h, wr_ref[...], preferred_element_type=F32, precision=lax.Precision.HIGHEST)
        logits = jnp.where(lane < N_EXPERTS, logits, -jnp.inf)
        v1 = jnp.max(logits, axis=-1, keepdims=True)
        i1 = jnp.min(jnp.where(logits == v1, lane, LANES), axis=-1, keepdims=True)
        rest = jnp.where(lane == i1, -jnp.inf, logits)
        v2 = jnp.max(rest, axis=-1, keepdims=True)
        i2 = jnp.min(jnp.where(rest == v2, lane, LANES), axis=-1, keepdims=True)
        p2 = jnp.exp(v2 - v1)
        w1 = 1.0 / (1.0 + p2)
        gate_sc[...] = jnp.where(lane == i1, w1, 0.0) + jnp.where(lane == i2, p2 * w1, 0.0)
        acc_sc[...] = jnp.zeros_like(acc_sc)

    hn = hn_sc[...]
    gate = jnp.sum(jnp.where(lane == e, gate_sc[...], 0.0), axis=-1, keepdims=True)
    a = _silu(_dot(hn, wg_ref[...])) * _dot(hn, wu_ref[...]) * gate
    acc_sc[...] += _dot(a.astype(BF16), wd_ref[...])

    @pl.when(e == pl.num_programs(1) - 1)
    def _():
        o_ref[...] = _rms(x_ref[...] + acc_sc[...], gf_ref[...])


def _moe_final(x2, g, w_router, w_gate, w_up, w_down, g_final):
    rows, dm = x2.shape
    tm = min(MOE_ROW_TILE, rows)
    ne, _, dff = w_gate.shape
    w_r = jnp.zeros((dm, LANES), F32).at[:, :ne].set(w_router)
    return pl.pallas_call(
        _moe_kernel,
        grid=(rows // tm, ne),
        in_specs=[pl.BlockSpec((tm, dm), lambda i, e: (i, 0)),
                  pl.BlockSpec((1, dm), lambda i, e: (0, 0)),
                  pl.BlockSpec((dm, LANES), lambda i, e: (0, 0)),
                  pl.BlockSpec((None, dm, dff), lambda i, e: (e, 0, 0)),
                  pl.BlockSpec((None, dm, dff), lambda i, e: (e, 0, 0)),
                  pl.BlockSpec((None, dff, dm), lambda i, e: (e, 0, 0)),
                  pl.BlockSpec((1, dm), lambda i, e: (0, 0))],
        out_specs=pl.BlockSpec((tm, dm), lambda i, e: (i, 0)),
        out_shape=jax.ShapeDtypeStruct((rows, dm), F32),
        scratch_shapes=[pltpu.VMEM((tm, dm), BF16), pltpu.VMEM((tm, LANES), F32), pltpu.VMEM((tm, dm), F32)],
        compiler_params=_params("parallel", "arbitrary"),
    )(x2, g.reshape(1, dm), w_r, w_gate.astype(BF16), w_up.astype(BF16), w_down.astype(BF16),
      g_final.reshape(1, dm))


def _rope_tables(seq):
    half = MLA_ROPE // 2
    inv = 1.0 / (ROPE_THETA ** (jnp.arange(0, MLA_ROPE, 2, dtype=F32) / MLA_ROPE))
    ang = jnp.arange(seq, dtype=F32)[:, None] * inv[None, :]
    reps = LANES // half
    return jnp.tile(jnp.cos(ang), (1, reps)), jnp.tile(jnp.sin(ang), (1, reps))


def kernel(x, ev_attn_norm, ev_w_in, ev_q_norm, ev_w_uq, ev_kv_norm, ev_w_ukv, ev_w_o, ev_ffn_norm, ev_w_gate, ev_w_up, ev_w_down, od_attn_norm, od_w_qkv, od_w_o, od_ffn_norm, od_w_router, od_w_gate, od_w_up, od_w_down, rel_bias, final_norm):
    b, s, dm = x.shape
    cos_t, sin_t = _rope_tables(s)

    q_a, k_a, v_a, q_b, k_b, v_b = _l0_proj(x, ev_attn_norm[0], ev_w_in[0], ev_q_norm[0], ev_w_uq[0],
                                            ev_kv_norm[0], ev_w_ukv[0], cos_t, sin_t)
    o_a = _mla_attention(q_a, k_a, v_a)
    o_b = _sb_attention(q_b, k_b, v_b)
    x = _out_proj([o_a, o_b], ev_w_o[0], x)
    x = _swiglu(x.reshape(b * s, dm), ev_ffn_norm[0], ev_w_gate[0], ev_w_up[0], ev_w_down[0]).reshape(b, s, dm)

    q, k_aug, v = _l1_proj(x, od_attn_norm[0], od_w_qkv[0], _moba_marks(s))
    o_c = _moba_attention(_moba_select(q, k_aug), k_aug, v, rel_bias)
    x = _out_proj([o_c], od_w_o[0], x)
    out = _moe_final(x.reshape(b * s, dm), od_ffn_norm[0], od_w_router[0], od_w_gate[0], od_w_up[0],
                     od_w_down[0], final_norm)
    return out.reshape(b, s, dm)
```

```python
import functools
import math

import jax
import jax.numpy as jnp
import numpy as np
from jax import lax
from jax.experimental import pallas as pl
from jax.experimental.pallas import tpu as pltpu

MLA_HEADS = 8
MLA_NOPE = 64
MLA_ROPE = 32
MLA_V = 64
Q_LORA = 384
KV_LORA = 256
ROPE_THETA = 10000.0
SB_HEADS = 8
SB_HEAD_DIM = 64
MOBA_HEADS = 16
MOBA_HEAD_DIM = 64
MOBA_BLOCK = 256
MOBA_TOPK = 3
REL_BUCKETS = 32
REL_MAX_EXACT = REL_BUCKETS // 2
REL_MAX_DIST = 128
N_EXPERTS = 8
TOP_K = 2
RMS_EPS = 1e-6

LANES = 128
VMEM_LIMIT = 56 * 1024 * 1024
ROW_TILE = 512
MOE_ROW_TILE = 1024
MLA_TILE = 2048
MLA_SUB = 256
MOBA_FAR = 8
MOBA_SELECT_TILES = 8
MOBA_SUB = 256
MOBA_MASK = 2.0 ** 100
SB_TILE = 256
SB_UNDERFLOW = -128.0
LOG2E = 1.4426950408889634

BF16 = jnp.bfloat16
F32 = jnp.float32


def _params(*sem):
    return pltpu.CompilerParams(dimension_semantics=sem, vmem_limit_bytes=VMEM_LIMIT)


def _rms(x, g):
    return x * lax.rsqrt(jnp.mean(x * x, axis=-1, keepdims=True) + RMS_EPS) * g


def _dot(a, b):
    return jnp.dot(a, b, preferred_element_type=F32)


def _dot_nt(a, b):
    return lax.dot_general(a, b, (((1,), (1,)), ((), ())), preferred_element_type=F32)


def _full(shape):
    return pl.BlockSpec(shape, lambda *_: (0,) * len(shape))


def _heads(y, h, d):
    return y[:, h * d:(h + 1) * d]


def _l0_proj_kernel(x_ref, g_ref, wcq_ref, wckv_ref, wkr_ref, wsb_ref, qn_ref, wuqn_ref, wuqr_ref,
                    kvn_ref, wukk_ref, wukv_ref, cos_ref, sin_ref,
                    qa_ref, ka_ref, va_ref, sbq_ref, sbk_ref, sbv_ref):
    hn = _rms(x_ref[...], g_ref[...]).astype(BF16)
    c = cos_ref[...]
    s = sin_ref[...]
    half = MLA_ROPE // 2

    def rope(t):
        t1 = t[:, :LANES]
        t2 = t[:, LANES:]
        return t1 * c - t2 * s, t2 * c + t1 * s

    cq = _rms(_dot(hn, wcq_ref[...]), qn_ref[...]).astype(BF16)
    qn = _dot(cq, wuqn_ref[...])
    r1, r2 = rope(_dot(cq, wuqr_ref[...]))
    ckv = _rms(_dot(hn, wckv_ref[...]), kvn_ref[...]).astype(BF16)
    kn = _dot(ckv, wukk_ref[...])
    va = _dot(ckv, wukv_ref[...])
    k1, k2 = rope(_dot(hn, wkr_ref[...]))
    k_rope = jnp.concatenate([k1[:, :half], k2[:, :half]], axis=-1).astype(BF16)
    for h in range(MLA_HEADS):
        qa_ref[h] = jnp.concatenate([_heads(qn, h, MLA_NOPE), _heads(r1, h, half), _heads(r2, h, half)],
                                    axis=-1).astype(BF16)
        ka_ref[h] = jnp.concatenate([_heads(kn, h, MLA_NOPE).astype(BF16), k_rope], axis=-1)
        va_ref[h] = _heads(va, h, MLA_V).astype(BF16)

    d = SB_HEADS * SB_HEAD_DIM
    sb = _dot(hn, wsb_ref[...])
    for h in range(SB_HEADS):
        sbq_ref[h] = (_heads(sb[:, :d], h, SB_HEAD_DIM) * (SB_HEAD_DIM ** -0.5)).astype(BF16)
        sbk_ref[h] = _heads(sb[:, d:2 * d], h, SB_HEAD_DIM).astype(BF16)
        sbv_ref[h] = _heads(sb[:, 2 * d:], h, SB_HEAD_DIM).astype(BF16)


def _l0_proj(x, g, w_in, q_norm, w_uq, kv_norm, w_ukv, cos_t, sin_t):
    b, seq, dm = x.shape
    tm = min(ROW_TILE, seq)
    half = MLA_ROPE // 2
    o_kr = Q_LORA + KV_LORA
    o_sb = o_kr + MLA_ROPE
    zpad = jnp.zeros((dm, LANES - half), F32)
    w_cq = w_in[:, :Q_LORA].astype(BF16)
    w_ckv = w_in[:, Q_LORA:o_kr].astype(BF16)
    w_kr = jnp.concatenate([w_in[:, o_kr:o_kr + half], zpad, w_in[:, o_kr + half:o_sb], zpad],
                           axis=1).astype(BF16)
    w_sb = w_in[:, o_sb:].astype(BF16)
    uq = w_uq.reshape(Q_LORA, MLA_HEADS, MLA_NOPE + MLA_ROPE)
    w_uqn = uq[:, :, :MLA_NOPE].reshape(Q_LORA, -1).astype(BF16)
    w_uqr = jnp.concatenate([uq[:, :, MLA_NOPE:MLA_NOPE + half].reshape(Q_LORA, -1),
                             uq[:, :, MLA_NOPE + half:].reshape(Q_LORA, -1)], axis=1).astype(BF16)
    ukv = w_ukv.reshape(KV_LORA, MLA_HEADS, MLA_NOPE + MLA_V)
    w_ukk = ukv[:, :, :MLA_NOPE].reshape(KV_LORA, -1).astype(BF16)
    w_ukv2 = ukv[:, :, MLA_NOPE:].reshape(KV_LORA, -1).astype(BF16)
    tab = pl.BlockSpec((tm, LANES), lambda bi, i: (i, 0))
    heads = lambda n, d: pl.BlockSpec((None, n, tm, d), lambda bi, i: (bi, 0, i, 0))
    out_dims = [(MLA_HEADS, MLA_NOPE + MLA_ROPE), (MLA_HEADS, MLA_NOPE + MLA_ROPE), (MLA_HEADS, MLA_V),
                (SB_HEADS, SB_HEAD_DIM), (SB_HEADS, SB_HEAD_DIM), (SB_HEADS, SB_HEAD_DIM)]
    return pl.pallas_call(
        _l0_proj_kernel,
        grid=(b, seq // tm),
        in_specs=[pl.BlockSpec((None, tm, dm), lambda bi, i: (bi, i, 0)), _full((1, dm)), _full(w_cq.shape),
                  _full(w_ckv.shape), _full(w_kr.shape), _full(w_sb.shape), _full((1, Q_LORA)),
                  _full(w_uqn.shape), _full(w_uqr.shape), _full((1, KV_LORA)), _full(w_ukk.shape),
                  _full(w_ukv2.shape), tab, tab],
        out_specs=[heads(n, d) for n, d in out_dims],
        out_shape=[jax.ShapeDtypeStruct((b, n, seq, d), BF16) for n, d in out_dims],
        compiler_params=_params("parallel", "parallel"),
    )(x, g.reshape(1, dm), w_cq, w_ckv, w_kr, w_sb, q_norm.reshape(1, -1), w_uqn, w_uqr,
      kv_norm.reshape(1, -1), w_ukk, w_ukv2, cos_t, sin_t)


def _mla_kernel(q_ref, k_ref, v_ref, o_ref, m_sc, l_sc, acc_sc, *, t, sub, c):
    qi = pl.program_id(2)
    m_sc[...] = jnp.full_like(m_sc, -jnp.inf)
    l_sc[...] = jnp.zeros_like(l_sc)
    acc_sc[...] = jnp.zeros_like(acc_sc)

    def tile(j, diagonal):
        off = pl.multiple_of(j * t, t)
        k = k_ref[pl.ds(off, t), :]
        v = v_ref[pl.ds(off, t), :]
        m_all, l_all, acc_all = m_sc[...], l_sc[...], acc_sc[...]
        m_out, l_out, acc_out = [], [], []
        widths = [r0 + sub if diagonal else t for r0 in range(0, t, sub)]
        logits = [_dot_nt(q_ref[pl.ds(r0, sub), :], k[:w]) for r0, w in zip(range(0, t, sub), widths)]
        for r0, width, s in zip(range(0, t, sub), widths, logits):
            if diagonal:
                r = lax.broadcasted_iota(jnp.int32, (sub, width), 0) + r0
                cc = lax.broadcasted_iota(jnp.int32, (sub, width), 1)
                s = jnp.where(cc <= r, s, -jnp.inf)
            m_prev = m_all[r0:r0 + sub]
            m_new = jnp.maximum(m_prev, jnp.max(s, axis=-1, keepdims=True))
            alpha = jnp.exp2((m_prev - m_new) * c)
            p = jnp.exp2((s - m_new) * c)
            l_out.append(alpha * l_all[r0:r0 + sub] + jnp.sum(p, axis=-1, keepdims=True))
            acc_out.append(alpha * acc_all[r0:r0 + sub] + _dot(p.astype(BF16), v[:width]))
            m_out.append(m_new)
        m_sc[...] = jnp.concatenate(m_out, axis=0)
        l_sc[...] = jnp.concatenate(l_out, axis=0)
        acc_sc[...] = jnp.concatenate(acc_out, axis=0)

    def body(j, carry):
        tile(j, False)
        return carry

    lax.fori_loop(0, qi, body, 0)
    tile(qi, True)
    o_ref[...] = (acc_sc[...] / l_sc[...]).astype(o_ref.dtype)


def _mla_attention(q, k, v):
    b, h, s, dq = q.shape
    dv = v.shape[-1]
    t = min(MLA_TILE, s)
    c = (MLA_NOPE + MLA_ROPE) ** -0.5 * LOG2E
    return pl.pallas_call(
        functools.partial(_mla_kernel, t=t, sub=min(MLA_SUB, t), c=c),
        grid=(b, h, s // t),
        in_specs=[pl.BlockSpec((None, None, t, dq), lambda bi, hi, i: (bi, hi, i, 0)),
                  pl.BlockSpec((None, None, s, dq), lambda bi, hi, i: (bi, hi, 0, 0)),
                  pl.BlockSpec((None, None, s, dv), lambda bi, hi, i: (bi, hi, 0, 0))],
        out_specs=pl.BlockSpec((None, None, t, dv), lambda bi, hi, i: (bi, hi, i, 0)),
        out_shape=jax.ShapeDtypeStruct((b, h, s, dv), BF16),
        scratch_shapes=[pltpu.VMEM((t, 1), F32), pltpu.VMEM((t, 1), F32), pltpu.VMEM((t, dv), F32)],
        compiler_params=_params("parallel", "parallel", "arbitrary"),
    )(q, k, v)


def _split3(x):
    hi = x.astype(BF16)
    r = x - hi.astype(F32)
    mid = r.astype(BF16)
    lo = (r - mid.astype(F32)).astype(BF16)
    return hi, mid, lo


def _sb_kernel(q_ref, k_ref, v_ref, o_ref, acc_sc, run_sc, *, t):
    qi = pl.program_id(2)
    q = q_ref[...]
    acc_sc[...] = jnp.zeros_like(acc_sc)
    run_sc[...] = jnp.zeros_like(run_sc)
    r = lax.broadcasted_iota(jnp.int32, (t, t), 0)
    cc = lax.broadcasted_iota(jnp.int32, (t, t), 1)
    later = (r > cc).astype(BF16)

    past = cc < r

    def scores(j):
        return _dot_nt(q, k_ref[pl.ds(pl.multiple_of(j * t, t), t), :])

    def log_terms(z, diagonal):
        log_fail = jnp.minimum(-z, 0.0) - jnp.log(1.0 + jnp.exp(-jnp.abs(z)))
        log_hit = log_fail + z
        if diagonal:
            log_fail = jnp.where(past, log_fail, 0.0)
        hi, mid, lo = _split3(log_fail)
        within = _dot(hi, later) + _dot(mid, later) + _dot(lo, later)
        return log_hit + within, jnp.sum(log_fail, axis=-1, keepdims=True)

    def weighted(j, log_w, run, diagonal):
        w = jnp.exp(log_w + run)
        if diagonal:
            w = jnp.where(past, w, 0.0)
        return _dot(w.astype(BF16), v_ref[pl.ds(pl.multiple_of(j * t, t), t), :])

    def tile(j, diagonal):
        log_w, total = log_terms(scores(j), diagonal)
        acc_sc[...] += weighted(j, log_w, run_sc[...], diagonal)
        run_sc[...] += total

    @pl.when(qi == 0)
    def _():
        tile(qi, True)

    @pl.when(qi > 0)
    def _():
        z_diag, z_prev = scores(qi), scores(qi - 1)
        lw_diag, total_diag = log_terms(z_diag, True)
        lw_prev, total_prev = log_terms(z_prev, False)
        acc_sc[...] = weighted(qi, lw_diag, 0.0, True) + weighted(qi - 1, lw_prev, total_diag, False)
        run_sc[...] = total_diag + total_prev

    def cond(carry):
        j, worst = carry
        return jnp.logical_and(j >= 0, worst > SB_UNDERFLOW)

    def body(carry):
        j, _ = carry
        tile(j, False)
        return j - 1, jnp.max(run_sc[...])

    lax.while_loop(cond, body, (qi - 2, jnp.max(run_sc[...])))
    o_ref[...] = acc_sc[...].astype(o_ref.dtype)


def _sb_attention(q, k, v):
    b, h, s, d = q.shape
    t = min(SB_TILE, s)
    return pl.pallas_call(
        functools.partial(_sb_kernel, t=t),
        grid=(b, h, s // t),
        in_specs=[pl.BlockSpec((None, None, t, d), lambda bi, hi, i: (bi, hi, i, 0)),
                  pl.BlockSpec((None, None, s, d), lambda bi, hi, i: (bi, hi, 0, 0)),
                  pl.BlockSpec((None, None, s, d), lambda bi, hi, i: (bi, hi, 0, 0))],
        out_specs=pl.BlockSpec((None, None, t, d), lambda bi, hi, i: (bi, hi, i, 0)),
        out_shape=jax.ShapeDtypeStruct((b, h, s, d), BF16),
        scratch_shapes=[pltpu.VMEM((t, d), F32), pltpu.VMEM((t, 1), F32)],
        compiler_params=_params("parallel", "parallel", "arbitrary"),
    )(q, k, v)


def _moba_select_kernel(q_ref, k_ref, qaug_ref, kmean_sc, *, nb, tiles):
    blk = MOBA_BLOCK
    d = MOBA_HEAD_DIM
    step = pl.program_id(2)

    @pl.when(step == 0)
    def _():
        kf = k_ref[:, pl.ds(0, d)].astype(F32).reshape(nb, blk, d)
        kmean_sc[...] = jnp.sum(kf, axis=1) * (1.0 / blk)

    k1, k2, k3 = _split3(kmean_sc[...])
    row = lax.broadcasted_iota(jnp.int32, (nb, blk), 0)
    eye = (lax.broadcasted_iota(jnp.int32, (nb, nb), 0) == lax.broadcasted_iota(jnp.int32, (nb, nb), 1))
    tile_rows = [pl.ds(t * blk, blk) for t in range(tiles)]
    gates = [_dot_nt(k1, q_ref[rows, :]) + _dot_nt(k2, q_ref[rows, :]) + _dot_nt(k3, q_ref[rows, :])
             for rows in tile_rows]
    for t, (rows, gate) in enumerate(zip(tile_rows, gates)):
        i = step * tiles + t
        q = q_ref[rows, :]
        gate = jnp.where(row < i, gate, -jnp.inf)
        masked = jnp.where(row == i, 0.0, 1.0)
        for _ in range(MOBA_TOPK):
            mx = jnp.max(gate, axis=0, keepdims=True)
            is_max = jnp.logical_and(gate == mx, mx > -jnp.inf)
            idx = jnp.min(jnp.where(is_max, row, nb), axis=0, keepdims=True)
            pick = row == idx
            masked = jnp.where(pick, 0.0, masked)
            gate = jnp.where(pick, -jnp.inf, gate)
        masked_t = lax.dot_general(masked.astype(BF16), eye.astype(BF16), (((0,), (0,)), ((), ())),
                                   preferred_element_type=F32)
        qaug_ref[rows, pl.ds(0, d)] = q
        qaug_ref[rows, pl.ds(d, nb)] = masked_t.astype(BF16)


def _moba_select(q, k_aug):
    b, h, s, d = q.shape
    nb = s // MOBA_BLOCK
    tiles = min(MOBA_SELECT_TILES, nb)
    t = tiles * MOBA_BLOCK
    return pl.pallas_call(
        functools.partial(_moba_select_kernel, nb=nb, tiles=tiles),
        grid=(b, h, s // t),
        in_specs=[pl.BlockSpec((None, None, t, d), lambda bi, hi, i: (bi, hi, i, 0)),
                  pl.BlockSpec((None, None, s, d + nb), lambda bi, hi, i: (bi, hi, 0, 0))],
        out_specs=pl.BlockSpec((None, None, t, d + nb), lambda bi, hi, i: (bi, hi, i, 0)),
        out_shape=jax.ShapeDtypeStruct((b, h, s, d + nb), BF16),
        scratch_shapes=[pltpu.VMEM((nb, d), F32)],
        compiler_params=_params("parallel", "parallel", "arbitrary"),
    )(q, k_aug)


def _moba_kernel(qaug_sc, k_ref, v_ref, bown_ref, bprev_ref, bfar_ref, o_ref,
                 m_sc, l_sc, acc_sc, la_sc, lb_sc, *, sub, far):
    blk = MOBA_BLOCK
    i = pl.program_id(2)
    row_tiles = [pl.ds(r0, sub) for r0 in range(0, blk, sub)]
    n_groups = (jnp.maximum(i, 1) - 1) // far

    tail = i - n_groups * far
    for w in range(far + 1):
        @pl.when(tail == w)
        def _(w=w):
            off = pl.multiple_of((i - w) * blk, blk)
            k = k_ref[pl.ds(off, (w + 1) * blk), :]
            v = v_ref[pl.ds(off, (w + 1) * blk), :]
            logits = [_dot_nt(qaug_sc[rows, :], k) for rows in row_tiles]
            k0 = k_ref[pl.ds(0, far * blk), :]
            for rows in row_tiles:
                la_sc[rows, :] = _dot_nt(qaug_sc[rows, :], k0)
            for rows, s in zip(row_tiles, logits):
                r = lax.broadcasted_iota(jnp.int32, (sub, blk), 0) + rows.start
                cc = lax.broadcasted_iota(jnp.int32, (sub, blk), 1)
                parts = [jnp.where(cc <= r, s[:, w * blk:] + bown_ref[rows, :], -jnp.inf)]
                if w >= 1:
                    parts.insert(0, s[:, (w - 1) * blk:w * blk] + bprev_ref[rows, :])
                if w >= 2:
                    parts.insert(0, s[:, :(w - 1) * blk] + bfar_ref[:, pl.ds(0, (w - 1) * blk)])
                s = parts[0] if len(parts) == 1 else jnp.concatenate(parts, axis=1)
                m0 = jnp.max(s, axis=-1, keepdims=True)
                p = jnp.exp(s - m0)
                m_sc[rows, :] = m0
                l_sc[rows, :] = jnp.sum(p, axis=-1, keepdims=True)
                acc_sc[rows, :] = _dot(p.astype(BF16), v)

    def update(logits, v, shift=None):
        m_all, l_all, acc_all = m_sc[...], l_sc[...], acc_sc[...]
        m_out, l_out, acc_out = [], [], []
        for rows, s in zip(row_tiles, logits):
            lo, hi = rows.start, rows.start + sub
            m_prev = m_all[lo:hi]
            m_cur = jnp.max(s, axis=-1, keepdims=True)
            if shift is None:
                m_new = jnp.maximum(m_prev, m_cur)
                p = jnp.exp(s - m_new)
            else:
                m_new = jnp.maximum(m_prev, m_cur + shift)
                p = jnp.exp(s - (m_new - shift))
            alpha = jnp.exp(m_prev - m_new)
            l_out.append(alpha * l_all[lo:hi] + jnp.sum(p, axis=-1, keepdims=True))
            acc_out.append(alpha * acc_all[lo:hi] + _dot(p.astype(BF16), v))
            m_out.append(m_new)
        m_sc[...] = jnp.concatenate(m_out, axis=0)
        l_sc[...] = jnp.concatenate(l_out, axis=0)
        acc_sc[...] = jnp.concatenate(acc_out, axis=0)

    def far_logits(g, buf):
        off = pl.multiple_of(g * (far * blk), far * blk)
        k = k_ref[pl.ds(off, far * blk), :]
        for rows in row_tiles:
            buf[rows, :] = _dot_nt(qaug_sc[rows, :], k)

    def far_consume(g, buf):
        off = pl.multiple_of(g * (far * blk), far * blk)
        update([buf[rows, :] for rows in row_tiles], v_ref[pl.ds(off, far * blk), :],
               shift=bfar_ref[:, pl.ds(0, 1)])

    last = n_groups - 1

    def far_pair(pair, carry):
        g = 2 * pair
        far_logits(g + 1, lb_sc)
        far_consume(g, la_sc)
        far_logits(g + 2, la_sc)
        far_consume(g + 1, lb_sc)
        return carry

    lax.fori_loop(0, lax.shift_right_logical(jnp.maximum(last, 0), 1), far_pair, 0)

    @pl.when(jnp.logical_and(n_groups > 0, (last & 1) == 0))
    def _():
        far_consume(last, la_sc)

    @pl.when(jnp.logical_and(n_groups > 0, (last & 1) == 1))
    def _():
        far_logits(last, lb_sc)
        far_consume(last - 1, la_sc)
        far_consume(last, lb_sc)

    o_ref[...] = (acc_sc[...] / l_sc[...]).astype(o_ref.dtype)


def _t5_bucket(dist):
    n = jnp.maximum(dist, 0)
    nf = jnp.maximum(n, 1).astype(F32)
    large = REL_MAX_EXACT + (jnp.log(nf / REL_MAX_EXACT) / math.log(REL_MAX_DIST / REL_MAX_EXACT)
                             * (REL_BUCKETS - REL_MAX_EXACT)).astype(jnp.int32)
    large = jnp.minimum(large, REL_BUCKETS - 1)
    return jnp.where(n < REL_MAX_EXACT, n, large)


def _moba_marks(seq):
    nb = seq // MOBA_BLOCK
    own_block = (jnp.arange(seq)[:, None] // MOBA_BLOCK) == jnp.arange(nb)[None, :]
    return jnp.where(own_block, -MOBA_MASK, 0.0).astype(BF16)


def _moba_attention(q_aug, k_aug, v, rel_bias):
    b, h, s, d = v.shape
    blk = MOBA_BLOCK
    assert s % blk == 0 and d == MOBA_HEAD_DIM
    assert 2 * blk > REL_MAX_DIST
    nb = s // blk
    far = min(MOBA_FAR, nb)
    offs = jnp.arange(blk)
    d_own = offs[:, None] - offs[None, :]

    def bias_table(dist):
        onehot = _t5_bucket(dist)[None, :, :] == jnp.arange(REL_BUCKETS)[:, None, None]
        return jnp.sum(jnp.where(onehot[None], rel_bias[:, :, None, None], 0.0), axis=1)

    b_own = bias_table(d_own)
    b_prev = bias_table(d_own + blk)
    b_far = jnp.broadcast_to(rel_bias[:, REL_BUCKETS - 1][:, None, None], (h, 1, far * blk))
    return pl.pallas_call(
        functools.partial(_moba_kernel, sub=MOBA_SUB, far=far),
        grid=(b, h, nb),
        in_specs=[pl.BlockSpec((None, None, blk, d + nb), lambda bi, hi, i: (bi, hi, i, 0)),
                  pl.BlockSpec((None, None, s, d + nb), lambda bi, hi, i: (bi, hi, 0, 0)),
                  pl.BlockSpec((None, None, s, d), lambda bi, hi, i: (bi, hi, 0, 0)),
                  pl.BlockSpec((None, blk, blk), lambda bi, hi, i: (hi, 0, 0)),
                  pl.BlockSpec((None, blk, blk), lambda bi, hi, i: (hi, 0, 0)),
                  pl.BlockSpec((None, 1, far * blk), lambda bi, hi, i: (hi, 0, 0))],
        out_specs=pl.BlockSpec((None, None, blk, d), lambda bi, hi, i: (bi, hi, i, 0)),
        out_shape=jax.ShapeDtypeStruct((b, h, s, d), BF16),
        scratch_shapes=[pltpu.VMEM((blk, 1), F32), pltpu.VMEM((blk, 1), F32), pltpu.VMEM((blk, d), F32),
                        pltpu.VMEM((blk, far * blk), F32), pltpu.VMEM((blk, far * blk), F32)],
        compiler_params=_params("parallel", "parallel", "arbitrary"),
    )(q_aug, k_aug, v, b_own, b_prev, b_far)


def _l1_proj_kernel(x_ref, g_ref, w_ref, marks_ref, q_ref, k_ref, v_ref):
    d = MOBA_HEAD_DIM
    width = MOBA_HEADS * d
    hn = _rms(x_ref[...], g_ref[...]).astype(BF16)
    y = _dot(hn, w_ref[...])
    marks = marks_ref[...]
    for h in range(MOBA_HEADS):
        q_ref[h] = (_heads(y[:, :width], h, d) * (d ** -0.5)).astype(BF16)
        k_ref[h] = jnp.concatenate([_heads(y[:, width:2 * width], h, d).astype(BF16), marks], axis=-1)
        v_ref[h] = _heads(y[:, 2 * width:], h, d).astype(BF16)


def _l1_proj(x, g, w_qkv, marks):
    b, seq, dm = x.shape
    tm = min(ROW_TILE, seq)
    nb = marks.shape[1]
    d = MOBA_HEAD_DIM
    heads = lambda n: pl.BlockSpec((None, MOBA_HEADS, tm, n), lambda bi, i: (bi, 0, i, 0))
    return pl.pallas_call(
        _l1_proj_kernel,
        grid=(b, seq // tm),
        in_specs=[pl.BlockSpec((None, tm, dm), lambda bi, i: (bi, i, 0)), _full((1, dm)), _full(w_qkv.shape),
                  pl.BlockSpec((tm, nb), lambda bi, i: (i, 0))],
        out_specs=[heads(d), heads(d + nb), heads(d)],
        out_shape=[jax.ShapeDtypeStruct((b, MOBA_HEADS, seq, n), BF16) for n in (d, d + nb, d)],
        compiler_params=_params("parallel", "parallel"),
    )(x, g.reshape(1, dm), w_qkv.astype(BF16), marks)


def _out_proj_kernel(*refs):
    *a_refs, w_ref, x_ref, o_ref = refs
    merged = jnp.concatenate([a_ref[h] for a_ref in a_refs for h in range(a_ref.shape[0])], axis=-1)
    o_ref[...] = x_ref[...] + _dot(merged, w_ref[...])


def _out_proj(mixed, w, x):
    b, seq, dm = x.shape
    tm = min(ROW_TILE, seq)
    row = pl.BlockSpec((None, tm, dm), lambda bi, i: (bi, i, 0))
    return pl.pallas_call(
        _out_proj_kernel,
        grid=(b, seq // tm),
        in_specs=[pl.BlockSpec((None, a.shape[1], tm, a.shape[3]), lambda bi, i: (bi, 0, i, 0)) for a in mixed]
        + [_full(w.shape), row],
        out_specs=row,
        out_shape=jax.ShapeDtypeStruct((b, seq, dm), F32),
        compiler_params=_params("parallel", "parallel"),
    )(*mixed, w.astype(BF16), x)


def _silu(x):
    return x / (1.0 + jnp.exp(-x))


def _swiglu_kernel(x_ref, g_ref, wg_ref, wu_ref, wd_ref, o_ref, *, chunk):
    x = x_ref[...]
    hn = _rms(x, g_ref[...]).astype(BF16)
    o_ref[...] = x
    for c0 in range(0, wg_ref.shape[1], chunk):
        a = _silu(_dot(hn, wg_ref[:, c0:c0 + chunk])) * _dot(hn, wu_ref[:, c0:c0 + chunk])
        o_ref[...] += _dot(a.astype(BF16), wd_ref[c0:c0 + chunk, :])


def _swiglu(x2, g, w_gate, w_up, w_down):
    rows, dm = x2.shape
    tm = min(ROW_TILE, rows)
    return pl.pallas_call(
        functools.partial(_swiglu_kernel, chunk=512),
        grid=(rows // tm,),
        in_specs=[pl.BlockSpec((tm, dm), lambda i: (i, 0)), _full((1, dm)), _full(w_gate.shape),
                  _full(w_up.shape), _full(w_down.shape)],
        out_specs=pl.BlockSpec((tm, dm), lambda i: (i, 0)),
        out_shape=jax.ShapeDtypeStruct((rows, dm), F32),
        compiler_params=_params("parallel"),
    )(x2, g.reshape(1, dm), w_gate.astype(BF16), w_up.astype(BF16), w_down.astype(BF16))


def _moe_kernel(x_ref, g_ref, wr_ref, wg_ref, wu_ref, wd_ref, gf_ref, o_ref, hn_sc, gate_sc, acc_sc):
    e = pl.program_id(1)
    lane = lax.broadcasted_iota(jnp.int32, gate_sc.shape, 1)

    @pl.when(e == 0)
    def _():
        h = _rms(x_ref[...], g_ref[...])
        hn_sc[...] = h.astype(BF16)
        logits = jnp.dot(h, wr_ref[...], preferred_element_type=F32, precision=lax.Precision.HIGHEST)
        logits = jnp.where(lane < N_EXPERTS, logits, -jnp.inf)
        v1 = jnp.max(logits, axis=-1, keepdims=True)
        i1 = jnp.min(jnp.where(logits == v1, lane, LANES), axis=-1, keepdims=True)
        rest = jnp.where(lane == i1, -jnp.inf, logits)
        v2 = jnp.max(rest, axis=-1, keepdims=True)
        i2 = jnp.min(jnp.where(rest == v2, lane, LANES), axis=-1, keepdims=True)
        p2 = jnp.exp(v2 - v1)
        w1 = 1.0 / (1.0 + p2)
        gate_sc[...] = jnp.where(lane == i1, w1, 0.0) + jnp.where(lane == i2, p2 * w1, 0.0)
        acc_sc[...] = jnp.zeros_like(acc_sc)

    gate = jnp.sum(jnp.where(lane == e, gate_sc[...], 0.0), axis=-1, keepdims=True)
    tm = hn_sc.shape[0]
    halves = [pl.ds(r0, tm // 2) for r0 in (0, tm // 2)]
    ups = [(_dot(hn_sc[rows, :], wg_ref[...]), _dot(hn_sc[rows, :], wu_ref[...])) for rows in halves]
    acc = acc_sc[...]
    outs = []
    for rows, (g, u) in zip(halves, ups):
        lo = rows.start
        a = _silu(g) * u * gate[lo:lo + tm // 2]
        outs.append(acc[lo:lo + tm // 2] + _dot(a.astype(BF16), wd_ref[...]))
    acc_sc[...] = jnp.concatenate(outs, axis=0)

    @pl.when(e == pl.num_programs(1) - 1)
    def _():
        o_ref[...] = _rms(x_ref[...] + acc_sc[...], gf_ref[...])


def _moe_final(x2, g, w_router, w_gate, w_up, w_down, g_final):
    rows, dm = x2.shape
    tm = min(MOE_ROW_TILE, rows)
    ne, _, dff = w_gate.shape
    w_r = jnp.zeros((dm, LANES), F32).at[:, :ne].set(w_router)
    return pl.pallas_call(
        _moe_kernel,
        grid=(rows // tm, ne),
        in_specs=[pl.BlockSpec((tm, dm), lambda i, e: (i, 0)),
                  pl.BlockSpec((1, dm), lambda i, e: (0, 0)),
                  pl.BlockSpec((dm, LANES), lambda i, e: (0, 0)),
                  pl.BlockSpec((None, dm, dff), lambda i, e: (e, 0, 0)),
                  pl.BlockSpec((None, dm, dff), lambda i, e: (e, 0, 0)),
                  pl.BlockSpec((None, dff, dm), lambda i, e: (e, 0, 0)),
                  pl.BlockSpec((1, dm), lambda i, e: (0, 0))],
        out_specs=pl.BlockSpec((tm, dm), lambda i, e: (i, 0)),
        out_shape=jax.ShapeDtypeStruct((rows, dm), F32),
        scratch_shapes=[pltpu.VMEM((tm, dm), BF16), pltpu.VMEM((tm, LANES), F32), pltpu.VMEM((tm, dm), F32)],
        compiler_params=_params("parallel", "arbitrary"),
    )(x2, g.reshape(1, dm), w_r, w_gate.astype(BF16), w_up.astype(BF16), w_down.astype(BF16),
      g_final.reshape(1, dm))


def _rope_tables(seq):
    half = MLA_ROPE // 2
    inv = 1.0 / (ROPE_THETA ** (jnp.arange(0, MLA_ROPE, 2, dtype=F32) / MLA_ROPE))
    ang = jnp.arange(seq, dtype=F32)[:, None] * inv[None, :]
    reps = LANES // half
    return jnp.tile(jnp.cos(ang), (1, reps)), jnp.tile(jnp.sin(ang), (1, reps))


def kernel(x, ev_attn_norm, ev_w_in, ev_q_norm, ev_w_uq, ev_kv_norm, ev_w_ukv, ev_w_o, ev_ffn_norm, ev_w_gate, ev_w_up, ev_w_down, od_attn_norm, od_w_qkv, od_w_o, od_ffn_norm, od_w_router, od_w_gate, od_w_up, od_w_down, rel_bias, final_norm):
    b, s, dm = x.shape
    cos_t, sin_t = _rope_tables(s)

    q_a, k_a, v_a, q_b, k_b, v_b = _l0_proj(x, ev_attn_norm[0], ev_w_in[0], ev_q_norm[0], ev_w_uq[0],
                                            ev_kv_norm[0], ev_w_ukv[0], cos_t, sin_t)
    o_a = _mla_attention(q_a, k_a, v_a)
    o_b = _sb_attention(q_b, k_b, v_b)
    x = _out_proj([o_a, o_b], ev_w_o[0], x)
    x = _swiglu(x.reshape(b * s, dm), ev_ffn_norm[0], ev_w_gate[0], ev_w_up[0], ev_w_down[0]).reshape(b, s, dm)

    q, k_aug, v = _l1_proj(x, od_attn_norm[0], od_w_qkv[0], _moba_marks(s))
    o_c = _moba_attention(_moba_select(q, k_aug), k_aug, v, rel_bias)
    x = _out_proj([o_c], od_w_o[0], x)
    out = _moe_final(x.reshape(b * s, dm), od_ffn_norm[0], od_w_router[0], od_w_gate[0], od_w_up[0],
                     od_w_down[0], final_norm)
    return out.reshape(b, s, dm)
```

```python
import functools
import math

import jax
import jax.numpy as jnp
import numpy as np
from jax import lax
from jax.experimental import pallas as pl
from jax.experimental.pallas import tpu as pltpu

MLA_HEADS = 8
MLA_NOPE = 64
MLA_ROPE = 32
MLA_V = 64
Q_LORA = 384
KV_LORA = 256
ROPE_THETA = 10000.0
SB_HEADS = 8
SB_HEAD_DIM = 64
MOBA_HEADS = 16
MOBA_HEAD_DIM = 64
MOBA_BLOCK = 256
MOBA_TOPK = 3
REL_BUCKETS = 32
REL_MAX_EXACT = REL_BUCKETS // 2
REL_MAX_DIST = 128
N_EXPERTS = 8
TOP_K = 2
RMS_EPS = 1e-6

LANES = 128
VMEM_LIMIT = 56 * 1024 * 1024
ROW_TILE = 512
MOE_ROW_TILE = 1024
MLA_TILE = 2048
MLA_SUB = 256
MOBA_FAR = 8
MOBA_SELECT_TILES = 8
MOBA_MASK = 2.0 ** 100
SB_TILE = 256
SB_UNDERFLOW = -128.0
LOG2E = 1.4426950408889634

BF16 = jnp.bfloat16
F32 = jnp.float32


def _params(*sem):
    return pltpu.CompilerParams(dimension_semantics=sem, vmem_limit_bytes=VMEM_LIMIT)


def _rms(x, g):
    return x * lax.rsqrt(jnp.mean(x * x, axis=-1, keepdims=True) + RMS_EPS) * g


def _dot(a, b):
    return jnp.dot(a, b, preferred_element_type=F32)


def _dot_nt(a, b):
    return lax.dot_general(a, b, (((1,), (1,)), ((), ())), preferred_element_type=F32)


def _full(shape):
    return pl.BlockSpec(shape, lambda *_: (0,) * len(shape))


def _heads(y, h, d):
    return y[:, h * d:(h + 1) * d]


def _l0_proj_kernel(x_ref, g_ref, wcq_ref, wckv_ref, wkr_ref, wsb_ref, qn_ref, wuqn_ref, wuqr_ref,
                    kvn_ref, wukk_ref, wukv_ref, cos_ref, sin_ref,
                    qa_ref, ka_ref, va_ref, sbq_ref, sbk_ref, sbv_ref):
    hn = _rms(x_ref[...], g_ref[...]).astype(BF16)
    c = cos_ref[...]
    s = sin_ref[...]
    half = MLA_ROPE // 2

    def rope(t):
        t1 = t[:, :LANES]
        t2 = t[:, LANES:]
        return t1 * c - t2 * s, t2 * c + t1 * s

    cq = _rms(_dot(hn, wcq_ref[...]), qn_ref[...]).astype(BF16)
    qn = _dot(cq, wuqn_ref[...])
    r1, r2 = rope(_dot(cq, wuqr_ref[...]))
    ckv = _rms(_dot(hn, wckv_ref[...]), kvn_ref[...]).astype(BF16)
    kn = _dot(ckv, wukk_ref[...])
    va = _dot(ckv, wukv_ref[...])
    k1, k2 = rope(_dot(hn, wkr_ref[...]))
    k_rope = jnp.concatenate([k1[:, :half], k2[:, :half]], axis=-1).astype(BF16)
    for h in range(MLA_HEADS):
        qa_ref[h] = jnp.concatenate([_heads(qn, h, MLA_NOPE), _heads(r1, h, half), _heads(r2, h, half)],
                                    axis=-1).astype(BF16)
        ka_ref[h] = jnp.concatenate([_heads(kn, h, MLA_NOPE).astype(BF16), k_rope], axis=-1)
        va_ref[h] = _heads(va, h, MLA_V).astype(BF16)

    d = SB_HEADS * SB_HEAD_DIM
    sb = _dot(hn, wsb_ref[...])
    for h in range(SB_HEADS):
        sbq_ref[h] = (_heads(sb[:, :d], h, SB_HEAD_DIM) * (SB_HEAD_DIM ** -0.5)).astype(BF16)
        sbk_ref[h] = _heads(sb[:, d:2 * d], h, SB_HEAD_DIM).astype(BF16)
        sbv_ref[h] = _heads(sb[:, 2 * d:], h, SB_HEAD_DIM).astype(BF16)


def _l0_proj(x, g, w_in, q_norm, w_uq, kv_norm, w_ukv, cos_t, sin_t):
    b, seq, dm = x.shape
    tm = min(ROW_TILE, seq)
    half = MLA_ROPE // 2
    o_kr = Q_LORA + KV_LORA
    o_sb = o_kr + MLA_ROPE
    zpad = jnp.zeros((dm, LANES - half), F32)
    w_cq = w_in[:, :Q_LORA].astype(BF16)
    w_ckv = w_in[:, Q_LORA:o_kr].astype(BF16)
    w_kr = jnp.concatenate([w_in[:, o_kr:o_kr + half], zpad, w_in[:, o_kr + half:o_sb], zpad],
                           axis=1).astype(BF16)
    w_sb = w_in[:, o_sb:].astype(BF16)
    uq = w_uq.reshape(Q_LORA, MLA_HEADS, MLA_NOPE + MLA_ROPE)
    w_uqn = uq[:, :, :MLA_NOPE].reshape(Q_LORA, -1).astype(BF16)
    w_uqr = jnp.concatenate([uq[:, :, MLA_NOPE:MLA_NOPE + half].reshape(Q_LORA, -1),
                             uq[:, :, MLA_NOPE + half:].reshape(Q_LORA, -1)], axis=1).astype(BF16)
    ukv = w_ukv.reshape(KV_LORA, MLA_HEADS, MLA_NOPE + MLA_V)
    w_ukk = ukv[:, :, :MLA_NOPE].reshape(KV_LORA, -1).astype(BF16)
    w_ukv2 = ukv[:, :, MLA_NOPE:].reshape(KV_LORA, -1).astype(BF16)
    tab = pl.BlockSpec((tm, LANES), lambda bi, i: (i, 0))
    heads = lambda n, d: pl.BlockSpec((None, n, tm, d), lambda bi, i: (bi, 0, i, 0))
    out_dims = [(MLA_HEADS, MLA_NOPE + MLA_ROPE), (MLA_HEADS, MLA_NOPE + MLA_ROPE), (MLA_HEADS, MLA_V),
                (SB_HEADS, SB_HEAD_DIM), (SB_HEADS, SB_HEAD_DIM), (SB_HEADS, SB_HEAD_DIM)]
    return pl.pallas_call(
        _l0_proj_kernel,
        grid=(b, seq // tm),
        in_specs=[pl.BlockSpec((None, tm, dm), lambda bi, i: (bi, i, 0)), _full((1, dm)), _full(w_cq.shape),
                  _full(w_ckv.shape), _full(w_kr.shape), _full(w_sb.shape), _full((1, Q_LORA)),
                  _full(w_uqn.shape), _full(w_uqr.shape), _full((1, KV_LORA)), _full(w_ukk.shape),
                  _full(w_ukv2.shape), tab, tab],
        out_specs=[heads(n, d) for n, d in out_dims],
        out_shape=[jax.ShapeDtypeStruct((b, n, seq, d), BF16) for n, d in out_dims],
        compiler_params=_params("parallel", "parallel"),
    )(x, g.reshape(1, dm), w_cq, w_ckv, w_kr, w_sb, q_norm.reshape(1, -1), w_uqn, w_uqr,
      kv_norm.reshape(1, -1), w_ukk, w_ukv2, cos_t, sin_t)


def _mla_kernel(q_ref, k_ref, v_ref, o_ref, m_sc, l_sc, acc_sc, *, t, sub, c):
    qi = pl.program_id(2)
    m_sc[...] = jnp.full_like(m_sc, -jnp.inf)
    l_sc[...] = jnp.zeros_like(l_sc)
    acc_sc[...] = jnp.zeros_like(acc_sc)

    def tile(j, diagonal):
        off = pl.multiple_of(j * t, t)
        k = k_ref[pl.ds(off, t), :]
        v = v_ref[pl.ds(off, t), :]
        m_all, l_all, acc_all = m_sc[...], l_sc[...], acc_sc[...]
        m_out, l_out, acc_out = [], [], []
        widths = [r0 + sub if diagonal else t for r0 in range(0, t, sub)]
        logits = [_dot_nt(q_ref[pl.ds(r0, sub), :], k[:w]) for r0, w in zip(range(0, t, sub), widths)]
        for r0, width, s in zip(range(0, t, sub), widths, logits):
            if diagonal:
                r = lax.broadcasted_iota(jnp.int32, (sub, width), 0) + r0
                cc = lax.broadcasted_iota(jnp.int32, (sub, width), 1)
                s = jnp.where(cc <= r, s, -jnp.inf)
            m_prev = m_all[r0:r0 + sub]
            m_new = jnp.maximum(m_prev, jnp.max(s, axis=-1, keepdims=True))
            alpha = jnp.exp2((m_prev - m_new) * c)
            p = jnp.exp2((s - m_new) * c)
            l_out.append(alpha * l_all[r0:r0 + sub] + jnp.sum(p, axis=-1, keepdims=True))
            acc_out.append(alpha * acc_all[r0:r0 + sub] + _dot(p.astype(BF16), v[:width]))
            m_out.append(m_new)
        m_sc[...] = jnp.concatenate(m_out, axis=0)
        l_sc[...] = jnp.concatenate(l_out, axis=0)
        acc_sc[...] = jnp.concatenate(acc_out, axis=0)

    def body(j, carry):
        tile(j, False)
        return carry

    lax.fori_loop(0, qi, body, 0)
    tile(qi, True)
    o_ref[...] = (acc_sc[...] / l_sc[...]).astype(o_ref.dtype)


def _mla_attention(q, k, v):
    b, h, s, dq = q.shape
    dv = v.shape[-1]
    t = min(MLA_TILE, s)
    c = (MLA_NOPE + MLA_ROPE) ** -0.5 * LOG2E
    return pl.pallas_call(
        functools.partial(_mla_kernel, t=t, sub=min(MLA_SUB, t), c=c),
        grid=(b, h, s // t),
        in_specs=[pl.BlockSpec((None, None, t, dq), lambda bi, hi, i: (bi, hi, i, 0)),
                  pl.BlockSpec((None, None, s, dq), lambda bi, hi, i: (bi, hi, 0, 0)),
                  pl.BlockSpec((None, None, s, dv), lambda bi, hi, i: (bi, hi, 0, 0))],
        out_specs=pl.BlockSpec((None, None, t, dv), lambda bi, hi, i: (bi, hi, i, 0)),
        out_shape=jax.ShapeDtypeStruct((b, h, s, dv), BF16),
        scratch_shapes=[pltpu.VMEM((t, 1), F32), pltpu.VMEM((t, 1), F32), pltpu.VMEM((t, dv), F32)],
        compiler_params=_params("parallel", "parallel", "arbitrary"),
    )(q, k, v)


def _split3(x):
    hi = x.astype(BF16)
    r = x - hi.astype(F32)
    mid = r.astype(BF16)
    lo = (r - mid.astype(F32)).astype(BF16)
    return hi, mid, lo


def _sb_kernel(q_ref, k_ref, v_ref, o_ref, acc_sc, run_sc, *, t):
    qi = pl.program_id(2)
    q = q_ref[...]
    acc_sc[...] = jnp.zeros_like(acc_sc)
    run_sc[...] = jnp.zeros_like(run_sc)
    r = lax.broadcasted_iota(jnp.int32, (t, t), 0)
    cc = lax.broadcasted_iota(jnp.int32, (t, t), 1)
    later = (r > cc).astype(BF16)

    past = cc < r

    def scores(j):
        return _dot_nt(q, k_ref[pl.ds(pl.multiple_of(j * t, t), t), :])

    def log_terms(z, diagonal):
        log_fail = jnp.minimum(-z, 0.0) - jnp.log(1.0 + jnp.exp(-jnp.abs(z)))
        log_hit = log_fail + z
        if diagonal:
            log_fail = jnp.where(past, log_fail, 0.0)
        hi, mid, lo = _split3(log_fail)
        within = _dot(hi, later) + _dot(mid, later) + _dot(lo, later)
        return log_hit + within, jnp.sum(log_fail, axis=-1, keepdims=True)

    def weighted(j, log_w, run, diagonal):
        w = jnp.exp(log_w + run)
        if diagonal:
            w = jnp.where(past, w, 0.0)
        return _dot(w.astype(BF16), v_ref[pl.ds(pl.multiple_of(j * t, t), t), :])

    def tile(j, diagonal):
        log_w, total = log_terms(scores(j), diagonal)
        acc_sc[...] += weighted(j, log_w, run_sc[...], diagonal)
        run_sc[...] += total

    @pl.when(qi == 0)
    def _():
        tile(qi, True)

    @pl.when(qi > 0)
    def _():
        z_diag, z_prev = scores(qi), scores(qi - 1)
        lw_diag, total_diag = log_terms(z_diag, True)
        lw_prev, total_prev = log_terms(z_prev, False)
        acc_sc[...] = weighted(qi, lw_diag, 0.0, True) + weighted(qi - 1, lw_prev, total_diag, False)
        run_sc[...] = total_diag + total_prev

    def cond(carry):
        j, worst = carry
        return jnp.logical_and(j >= 0, worst > SB_UNDERFLOW)

    def body(carry):
        j, _ = carry
        tile(j, False)
        return j - 1, jnp.max(run_sc[...])

    lax.while_loop(cond, body, (qi - 2, jnp.max(run_sc[...])))
    o_ref[...] = acc_sc[...].astype(o_ref.dtype)


def _sb_attention(q, k, v):
    b, h, s, d = q.shape
    t = min(SB_TILE, s)
    return pl.pallas_call(
        functools.partial(_sb_kernel, t=t),
        grid=(b, h, s // t),
        in_specs=[pl.BlockSpec((None, None, t, d), lambda bi, hi, i: (bi, hi, i, 0)),
                  pl.BlockSpec((None, None, s, d), lambda bi, hi, i: (bi, hi, 0, 0)),
                  pl.BlockSpec((None, None, s, d), lambda bi, hi, i: (bi, hi, 0, 0))],
        out_specs=pl.BlockSpec((None, None, t, d), lambda bi, hi, i: (bi, hi, i, 0)),
        out_shape=jax.ShapeDtypeStruct((b, h, s, d), BF16),
        scratch_shapes=[pltpu.VMEM((t, d), F32), pltpu.VMEM((t, 1), F32)],
        compiler_params=_params("parallel", "parallel", "arbitrary"),
    )(q, k, v)


def _moba_select_kernel(q_ref, k_ref, qaug_ref, kmean_sc, *, nb, tiles):
    blk = MOBA_BLOCK
    d = MOBA_HEAD_DIM
    step = pl.program_id(2)

    @pl.when(step == 0)
    def _():
        kf = k_ref[:, pl.ds(0, d)].astype(F32).reshape(nb, blk, d)
        kmean_sc[...] = jnp.sum(kf, axis=1) * (1.0 / blk)

    k1, k2, k3 = _split3(kmean_sc[...])
    row = lax.broadcasted_iota(jnp.int32, (nb, blk), 0)
    eye = (lax.broadcasted_iota(jnp.int32, (nb, nb), 0) == lax.broadcasted_iota(jnp.int32, (nb, nb), 1))
    tile_rows = [pl.ds(t * blk, blk) for t in range(tiles)]
    gates = [_dot_nt(k1, q_ref[rows, :]) + _dot_nt(k2, q_ref[rows, :]) + _dot_nt(k3, q_ref[rows, :])
             for rows in tile_rows]
    for t, (rows, gate) in enumerate(zip(tile_rows, gates)):
        i = step * tiles + t
        q = q_ref[rows, :]
        gate = jnp.where(row < i, gate, -jnp.inf)
        masked = jnp.where(row == i, 0.0, 1.0)
        for _ in range(MOBA_TOPK):
            mx = jnp.max(gate, axis=0, keepdims=True)
            is_max = jnp.logical_and(gate == mx, mx > -jnp.inf)
            idx = jnp.min(jnp.where(is_max, row, nb), axis=0, keepdims=True)
            pick = row == idx
            masked = jnp.where(pick, 0.0, masked)
            gate = jnp.where(pick, -jnp.inf, gate)
        masked_t = lax.dot_general(masked.astype(BF16), eye.astype(BF16), (((0,), (0,)), ((), ())),
                                   preferred_element_type=F32)
        qaug_ref[rows, pl.ds(0, d)] = q
        qaug_ref[rows, pl.ds(d, nb)] = masked_t.astype(BF16)


def _moba_select(q, k_aug):
    b, h, s, d = q.shape
    nb = s // MOBA_BLOCK
    tiles = min(MOBA_SELECT_TILES, nb)
    t = tiles * MOBA_BLOCK
    return pl.pallas_call(
        functools.partial(_moba_select_kernel, nb=nb, tiles=tiles),
        grid=(b, h, s // t),
        in_specs=[pl.BlockSpec((None, None, t, d), lambda bi, hi, i: (bi, hi, i, 0)),
                  pl.BlockSpec((None, None, s, d + nb), lambda bi, hi, i: (bi, hi, 0, 0))],
        out_specs=pl.BlockSpec((None, None, t, d + nb), lambda bi, hi, i: (bi, hi, i, 0)),
        out_shape=jax.ShapeDtypeStruct((b, h, s, d + nb), BF16),
        scratch_shapes=[pltpu.VMEM((nb, d), F32)],
        compiler_params=_params("parallel", "parallel", "arbitrary"),
    )(q, k_aug)


def _moba_kernel(qaug_ref, k_ref, vt_ref, bown_ref, bprev_ref, bfar_ref, o_ref,
                 m_sc, l_sc, acc_sc, la_sc, lb_sc, *, far):
    blk = MOBA_BLOCK
    i = pl.program_id(2)
    q = qaug_ref[...]
    n_groups = (jnp.maximum(i, 1) - 1) // far

    def logits(off, n):
        if n < 2:
            return _dot_nt(k_ref[pl.ds(off, n * blk), :], q)
        n0 = n // 2
        return jnp.concatenate([_dot_nt(k_ref[pl.ds(off, n0 * blk), :], q),
                                _dot_nt(k_ref[pl.ds(off + n0 * blk, (n - n0) * blk), :], q)], axis=0)

    tail = i - n_groups * far
    for w in range(far + 1):
        @pl.when(tail == w)
        def _(w=w):
            off = pl.multiple_of((i - w) * blk, blk)
            s = logits(off, w + 1)
            la_sc[...] = logits(0, far)
            key = lax.broadcasted_iota(jnp.int32, (blk, blk), 0)
            qry = lax.broadcasted_iota(jnp.int32, (blk, blk), 1)
            parts = [jnp.where(key <= qry, s[w * blk:] + bown_ref[...], -jnp.inf)]
            if w >= 1:
                parts.insert(0, s[(w - 1) * blk:w * blk] + bprev_ref[...])
            if w >= 2:
                parts.insert(0, s[:(w - 1) * blk] + bfar_ref[...])
            s = parts[0] if len(parts) == 1 else jnp.concatenate(parts, axis=0)
            m0 = jnp.max(s, axis=0, keepdims=True)
            p = jnp.exp(s - m0)
            m_sc[...] = m0
            l_sc[...] = jnp.sum(p, axis=0, keepdims=True)
            acc_sc[...] = _dot(vt_ref[:, pl.ds(off, (w + 1) * blk)], p.astype(BF16))

    def far_logits(g, buf):
        buf[...] = logits(pl.multiple_of(g * (far * blk), far * blk), far)

    def far_consume(g, buf):
        off = pl.multiple_of(g * (far * blk), far * blk)
        s = buf[...]
        shift = bfar_ref[...]
        m_prev = m_sc[...]
        m_new = jnp.maximum(m_prev, jnp.max(s, axis=0, keepdims=True) + shift)
        alpha = jnp.exp(m_prev - m_new)
        ref_point = m_new - shift
        l_new = alpha * l_sc[...]
        acc_new = alpha * acc_sc[...]
        for n in range(far):
            p = jnp.exp(s[n * blk:(n + 1) * blk] - ref_point)
            l_new = l_new + jnp.sum(p, axis=0, keepdims=True)
            acc_new = acc_new + _dot(vt_ref[:, pl.ds(off + n * blk, blk)], p.astype(BF16))
        l_sc[...] = l_new
        acc_sc[...] = acc_new
        m_sc[...] = m_new

    last = n_groups - 1

    def far_pair(pair, carry):
        g = 2 * pair
        far_logits(g + 1, lb_sc)
        far_consume(g, la_sc)
        far_logits(g + 2, la_sc)
        far_consume(g + 1, lb_sc)
        return carry

    lax.fori_loop(0, lax.shift_right_logical(jnp.maximum(last, 0), 1), far_pair, 0)

    @pl.when(jnp.logical_and(n_groups > 0, (last & 1) == 0))
    def _():
        far_consume(last, la_sc)

    @pl.when(jnp.logical_and(n_groups > 0, (last & 1) == 1))
    def _():
        far_logits(last, lb_sc)
        far_consume(last - 1, la_sc)
        far_consume(last, lb_sc)

    o_ref[...] = (acc_sc[...] / l_sc[...]).astype(o_ref.dtype)


def _t5_bucket(dist):
    n = jnp.maximum(dist, 0)
    nf = jnp.maximum(n, 1).astype(F32)
    large = REL_MAX_EXACT + (jnp.log(nf / REL_MAX_EXACT) / math.log(REL_MAX_DIST / REL_MAX_EXACT)
                             * (REL_BUCKETS - REL_MAX_EXACT)).astype(jnp.int32)
    large = jnp.minimum(large, REL_BUCKETS - 1)
    return jnp.where(n < REL_MAX_EXACT, n, large)


def _moba_marks(seq):
    nb = seq // MOBA_BLOCK
    own_block = (jnp.arange(seq)[:, None] // MOBA_BLOCK) == jnp.arange(nb)[None, :]
    return jnp.where(own_block, -MOBA_MASK, 0.0).astype(BF16)


def _moba_attention(q_aug, k_aug, v_t, rel_bias):
    b, h, d, s = v_t.shape
    blk = MOBA_BLOCK
    assert s % blk == 0 and d == MOBA_HEAD_DIM
    assert 2 * blk > REL_MAX_DIST
    nb = s // blk
    far = min(MOBA_FAR, nb)
    offs = jnp.arange(blk)
    dist = offs[None, :] - offs[:, None]

    def bias_table(dist):
        onehot = _t5_bucket(dist)[None, :, :] == jnp.arange(REL_BUCKETS)[:, None, None]
        return jnp.sum(jnp.where(onehot[None], rel_bias[:, :, None, None], 0.0), axis=1)

    b_own = bias_table(dist)
    b_prev = bias_table(dist + blk)
    b_far = jnp.broadcast_to(rel_bias[:, REL_BUCKETS - 1][:, None, None], (h, 1, blk))
    return pl.pallas_call(
        functools.partial(_moba_kernel, far=far),
        grid=(b, h, nb),
        in_specs=[pl.BlockSpec((None, None, blk, d + nb), lambda bi, hi, i: (bi, hi, i, 0)),
                  pl.BlockSpec((None, None, s, d + nb), lambda bi, hi, i: (bi, hi, 0, 0)),
                  pl.BlockSpec((None, None, d, s), lambda bi, hi, i: (bi, hi, 0, 0)),
                  pl.BlockSpec((None, blk, blk), lambda bi, hi, i: (hi, 0, 0)),
                  pl.BlockSpec((None, blk, blk), lambda bi, hi, i: (hi, 0, 0)),
                  pl.BlockSpec((None, 1, blk), lambda bi, hi, i: (hi, 0, 0))],
        out_specs=pl.BlockSpec((None, None, d, blk), lambda bi, hi, i: (bi, hi, 0, i)),
        out_shape=jax.ShapeDtypeStruct((b, h, d, s), BF16),
        scratch_shapes=[pltpu.VMEM((1, blk), F32), pltpu.VMEM((1, blk), F32), pltpu.VMEM((d, blk), F32),
                        pltpu.VMEM((far * blk, blk), F32), pltpu.VMEM((far * blk, blk), F32)],
        compiler_params=_params("parallel", "parallel", "arbitrary"),
    )(q_aug, k_aug, v_t, b_own, b_prev, b_far)


def _l1_proj_kernel(x_ref, g_ref, wqk_ref, wvt_ref, marks_ref, q_ref, k_ref, vt_ref):
    d = MOBA_HEAD_DIM
    width = MOBA_HEADS * d
    hn = _rms(x_ref[...], g_ref[...]).astype(BF16)
    y = _dot(hn, wqk_ref[...])
    v_t = _dot_nt(wvt_ref[...], hn)
    marks = marks_ref[...]
    for h in range(MOBA_HEADS):
        q_ref[h] = (_heads(y[:, :width], h, d) * (d ** -0.5)).astype(BF16)
        k_ref[h] = jnp.concatenate([_heads(y[:, width:], h, d).astype(BF16), marks], axis=-1)
        vt_ref[h] = v_t[h * d:(h + 1) * d].astype(BF16)


def _l1_proj(x, g, w_qkv, marks):
    b, seq, dm = x.shape
    tm = min(ROW_TILE, seq)
    nb = marks.shape[1]
    d = MOBA_HEAD_DIM
    width = MOBA_HEADS * d
    w_qk = w_qkv[:, :2 * width].astype(BF16)
    w_vt = w_qkv[:, 2 * width:].T.astype(BF16)
    heads = lambda n: pl.BlockSpec((None, MOBA_HEADS, tm, n), lambda bi, i: (bi, 0, i, 0))
    return pl.pallas_call(
        _l1_proj_kernel,
        grid=(b, seq // tm),
        in_specs=[pl.BlockSpec((None, tm, dm), lambda bi, i: (bi, i, 0)), _full((1, dm)), _full(w_qk.shape),
                  _full(w_vt.shape), pl.BlockSpec((tm, nb), lambda bi, i: (i, 0))],
        out_specs=[heads(d), heads(d + nb),
                   pl.BlockSpec((None, MOBA_HEADS, d, tm), lambda bi, i: (bi, 0, 0, i))],
        out_shape=[jax.ShapeDtypeStruct((b, MOBA_HEADS, seq, d), BF16),
                   jax.ShapeDtypeStruct((b, MOBA_HEADS, seq, d + nb), BF16),
                   jax.ShapeDtypeStruct((b, MOBA_HEADS, d, seq), BF16)],
        compiler_params=_params("parallel", "parallel"),
    )(x, g.reshape(1, dm), w_qk, w_vt, marks)


def _out_proj_kernel(*refs, transposed):
    *a_refs, w_ref, x_ref, o_ref = refs
    if transposed:
        (a_ref,) = a_refs
        heads, d, rows = a_ref.shape
        merged_t = a_ref[...].reshape(heads * d, rows)
        y = lax.dot_general(merged_t, w_ref[...], (((0,), (0,)), ((), ())), preferred_element_type=F32)
    else:
        merged = jnp.concatenate([a_ref[h] for a_ref in a_refs for h in range(a_ref.shape[0])], axis=-1)
        y = _dot(merged, w_ref[...])
    o_ref[...] = x_ref[...] + y


def _out_proj(mixed, w, x, transposed=False):
    b, seq, dm = x.shape
    tm = min(ROW_TILE, seq)
    row = pl.BlockSpec((None, tm, dm), lambda bi, i: (bi, i, 0))
    if transposed:
        specs = [pl.BlockSpec((None, a.shape[1], a.shape[2], tm), lambda bi, i: (bi, 0, 0, i)) for a in mixed]
    else:
        specs = [pl.BlockSpec((None, a.shape[1], tm, a.shape[3]), lambda bi, i: (bi, 0, i, 0)) for a in mixed]
    return pl.pallas_call(
        functools.partial(_out_proj_kernel, transposed=transposed),
        grid=(b, seq // tm),
        in_specs=specs + [_full(w.shape), row],
        out_specs=row,
        out_shape=jax.ShapeDtypeStruct((b, seq, dm), F32),
        compiler_params=_params("parallel", "parallel"),
    )(*mixed, w.astype(BF16), x)


def _silu(x):
    return x / (1.0 + jnp.exp(-x))


def _swiglu_kernel(x_ref, g_ref, wg_ref, wu_ref, wd_ref, o_ref, *, chunk):
    x = x_ref[...]
    hn = _rms(x, g_ref[...]).astype(BF16)
    o_ref[...] = x
    for c0 in range(0, wg_ref.shape[1], chunk):
        a = _silu(_dot(hn, wg_ref[:, c0:c0 + chunk])) * _dot(hn, wu_ref[:, c0:c0 + chunk])
        o_ref[...] += _dot(a.astype(BF16), wd_ref[c0:c0 + chunk, :])


def _swiglu(x2, g, w_gate, w_up, w_down):
    rows, dm = x2.shape
    tm = min(ROW_TILE, rows)
    return pl.pallas_call(
        functools.partial(_swiglu_kernel, chunk=512),
        grid=(rows // tm,),
        in_specs=[pl.BlockSpec((tm, dm), lambda i: (i, 0)), _full((1, dm)), _full(w_gate.shape),
                  _full(w_up.shape), _full(w_down.shape)],
        out_specs=pl.BlockSpec((tm, dm), lambda i: (i, 0)),
        out_shape=jax.ShapeDtypeStruct((rows, dm), F32),
        compiler_params=_params("parallel"),
    )(x2, g.reshape(1, dm), w_gate.astype(BF16), w_up.astype(BF16), w_down.astype(BF16))


def _moe_kernel(x_ref, g_ref, wr_ref, wg_ref, wu_ref, wd_ref, gf_ref, o_ref, hn_sc, gate_sc, acc_sc):
    e = pl.program_id(1)
    lane = lax.broadcasted_iota(jnp.int32, gate_sc.shape, 1)

    @pl.when(e == 0)
    def _():
        h = _rms(x_ref[...], g_ref[...])
        hn_sc[...] = h.astype(BF16)
        logits = jnp.dot(h, wr_ref[...], preferred_element_type=F32, precision=lax.Precision.HIGHEST)
        logits = jnp.where(lane < N_EXPERTS, logits, -jnp.inf)
        v1 = jnp.max(logits, axis=-1, keepdims=True)
        i1 = jnp.min(jnp.where(logits == v1, lane, LANES), axis=-1, keepdims=True)
        rest = jnp.where(lane == i1, -jnp.inf, logits)
        v2 = jnp.max(rest, axis=-1, keepdims=True)
        i2 = jnp.min(jnp.where(rest == v2, lane, LANES), axis=-1, keepdims=True)
        p2 = jnp.exp(v2 - v1)
        w1 = 1.0 / (1.0 + p2)
        gate_sc[...] = jnp.where(lane == i1, w1, 0.0) + jnp.where(lane == i2, p2 * w1, 0.0)
        acc_sc[...] = jnp.zeros_like(acc_sc)

    gate = jnp.sum(jnp.where(lane == e, gate_sc[...], 0.0), axis=-1, keepdims=True)
    tm = hn_sc.shape[0]
    halves = [pl.ds(r0, tm // 2) for r0 in (0, tm // 2)]
    ups = [(_dot(hn_sc[rows, :], wg_ref[...]), _dot(hn_sc[rows, :], wu_ref[...])) for rows in halves]
    acc = acc_sc[...]
    outs = []
    for rows, (g, u) in zip(halves, ups):
        lo = rows.start
        a = _silu(g) * u * gate[lo:lo + tm // 2]
        outs.append(acc[lo:lo + tm // 2] + _dot(a.astype(BF16), wd_ref[...]))
    acc_sc[...] = jnp.concatenate(outs, axis=0)

    @pl.when(e == pl.num_programs(1) - 1)
    def _():
        o_ref[...] = _rms(x_ref[...] + acc_sc[...], gf_ref[...])


def _moe_final(x2, g, w_router, w_gate, w_up, w_down, g_final):
    rows, dm = x2.shape
    tm = min(MOE_ROW_TILE, rows)
    ne, _, dff = w_gate.shape
    w_r = jnp.zeros((dm, LANES), F32).at[:, :ne].set(w_router)
    return pl.pallas_call(
        _moe_kernel,
        grid=(rows // tm, ne),
        in_specs=[pl.BlockSpec((tm, dm), lambda i, e: (i, 0)),
                  pl.BlockSpec((1, dm), lambda i, e: (0, 0)),
                  pl.BlockSpec((dm, LANES), lambda i, e: (0, 0)),
                  pl.BlockSpec((None, dm, dff), lambda i, e: (e, 0, 0)),
                  pl.BlockSpec((None, dm, dff), lambda i, e: (e, 0, 0)),
                  pl.BlockSpec((None, dff, dm), lambda i, e: (e, 0, 0)),
                  pl.BlockSpec((1, dm), lambda i, e: (0, 0))],
        out_specs=pl.BlockSpec((tm, dm), lambda i, e: (i, 0)),
        out_shape=jax.ShapeDtypeStruct((rows, dm), F32),
        scratch_shapes=[pltpu.VMEM((tm, dm), BF16), pltpu.VMEM((tm, LANES), F32), pltpu.VMEM((tm, dm), F32)],
        compiler_params=_params("parallel", "arbitrary"),
    )(x2, g.reshape(1, dm), w_r, w_gate.astype(BF16), w_up.astype(BF16), w_down.astype(BF16),
      g_final.reshape(1, dm))


def _rope_tables(seq):
    half = MLA_ROPE // 2
    inv = 1.0 / (ROPE_THETA ** (jnp.arange(0, MLA_ROPE, 2, dtype=F32) / MLA_ROPE))
    ang = jnp.arange(seq, dtype=F32)[:, None] * inv[None, :]
    reps = LANES // half
    return jnp.tile(jnp.cos(ang), (1, reps)), jnp.tile(jnp.sin(ang), (1, reps))


def kernel(x, ev_attn_norm, ev_w_in, ev_q_norm, ev_w_uq, ev_kv_norm, ev_w_ukv, ev_w_o, ev_ffn_norm, ev_w_gate, ev_w_up, ev_w_down, od_attn_norm, od_w_qkv, od_w_o, od_ffn_norm, od_w_router, od_w_gate, od_w_up, od_w_down, rel_bias, final_norm):
    b, s, dm = x.shape
    cos_t, sin_t = _rope_tables(s)

    q_a, k_a, v_a, q_b, k_b, v_b = _l0_proj(x, ev_attn_norm[0], ev_w_in[0], ev_q_norm[0], ev_w_uq[0],
                                            ev_kv_norm[0], ev_w_ukv[0], cos_t, sin_t)
    o_a = _mla_attention(q_a, k_a, v_a)
    o_b = _sb_attention(q_b, k_b, v_b)
    x = _out_proj([o_a, o_b], ev_w_o[0], x)
    x = _swiglu(x.reshape(b * s, dm), ev_ffn_norm[0], ev_w_gate[0], ev_w_up[0], ev_w_down[0]).reshape(b, s, dm)

    q, k_aug, v_t = _l1_proj(x, od_attn_norm[0], od_w_qkv[0], _moba_marks(s))
    o_c_t = _moba_attention(_moba_select(q, k_aug), k_aug, v_t, rel_bias)
    x = _out_proj([o_c_t], od_w_o[0], x, transposed=True)
    out = _moe_final(x.reshape(b * s, dm), od_ffn_norm[0], od_w_router[0], od_w_gate[0], od_w_up[0],
                     od_w_down[0], final_norm)
    return out.reshape(b, s, dm)
```

```python
import functools
import math

import jax
import jax.numpy as jnp
import numpy as np
from jax import lax
from jax.experimental import pallas as pl
from jax.experimental.pallas import tpu as pltpu

MLA_HEADS = 8
MLA_NOPE = 64
MLA_ROPE = 32
MLA_V = 64
Q_LORA = 384
KV_LORA = 256
ROPE_THETA = 10000.0
SB_HEADS = 8
SB_HEAD_DIM = 64
MOBA_HEADS = 16
MOBA_HEAD_DIM = 64
MOBA_BLOCK = 256
MOBA_TOPK = 3
REL_BUCKETS = 32
REL_MAX_EXACT = REL_BUCKETS // 2
REL_MAX_DIST = 128
N_EXPERTS = 8
TOP_K = 2
RMS_EPS = 1e-6

LANES = 128
VMEM_LIMIT = 56 * 1024 * 1024
ROW_TILE = 512
MOE_ROW_TILE = 1024
MLA_TILE = 2048
MLA_SUB = 256
MOBA_FAR = 8
MOBA_SELECT_TILES = 8
MOBA_MASK = 2.0 ** 100
SB_TILE = 256
SB_UNDERFLOW = -128.0
LOG2E = 1.4426950408889634

BF16 = jnp.bfloat16
F32 = jnp.float32


def _params(*sem):
    return pltpu.CompilerParams(dimension_semantics=sem, vmem_limit_bytes=VMEM_LIMIT)


def _rms(x, g):
    return x * lax.rsqrt(jnp.mean(x * x, axis=-1, keepdims=True) + RMS_EPS) * g


def _dot(a, b):
    return jnp.dot(a, b, preferred_element_type=F32)


def _dot_nt(a, b):
    return lax.dot_general(a, b, (((1,), (1,)), ((), ())), preferred_element_type=F32)


def _full(shape):
    return pl.BlockSpec(shape, lambda *_: (0,) * len(shape))


def _heads(y, h, d):
    return y[:, h * d:(h + 1) * d]


def _l0_proj_kernel(x_ref, g_ref, wcq_ref, wckv_ref, wkr_ref, wsb_ref, qn_ref, wuqn_ref, wuqr_ref,
                    kvn_ref, wukk_ref, wukv_ref, cos_ref, sin_ref,
                    qa_ref, ka_ref, va_ref, sbq_ref, sbk_ref, sbv_ref):
    hn = _rms(x_ref[...], g_ref[...]).astype(BF16)
    c = cos_ref[...]
    s = sin_ref[...]
    half = MLA_ROPE // 2

    def rope(t):
        t1 = t[:, :LANES]
        t2 = t[:, LANES:]
        return t1 * c - t2 * s, t2 * c + t1 * s

    cq = _rms(_dot(hn, wcq_ref[...]), qn_ref[...]).astype(BF16)
    qn = _dot(cq, wuqn_ref[...])
    r1, r2 = rope(_dot(cq, wuqr_ref[...]))
    ckv = _rms(_dot(hn, wckv_ref[...]), kvn_ref[...]).astype(BF16)
    kn = _dot(ckv, wukk_ref[...])
    va_t = _dot_nt(wukv_ref[...], ckv)
    k1, k2 = rope(_dot(hn, wkr_ref[...]))
    k_rope = jnp.concatenate([k1[:, :half], k2[:, :half]], axis=-1).astype(BF16)
    for h in range(MLA_HEADS):
        qa_ref[h] = jnp.concatenate([_heads(qn, h, MLA_NOPE), _heads(r1, h, half), _heads(r2, h, half)],
                                    axis=-1).astype(BF16)
        ka_ref[h] = jnp.concatenate([_heads(kn, h, MLA_NOPE).astype(BF16), k_rope], axis=-1)
        va_ref[h] = va_t[h * MLA_V:(h + 1) * MLA_V].astype(BF16)

    d = SB_HEADS * SB_HEAD_DIM
    sb = _dot(hn, wsb_ref[...])
    for h in range(SB_HEADS):
        sbq_ref[h] = (_heads(sb[:, :d], h, SB_HEAD_DIM) * (SB_HEAD_DIM ** -0.5)).astype(BF16)
        sbk_ref[h] = _heads(sb[:, d:2 * d], h, SB_HEAD_DIM).astype(BF16)
        sbv_ref[h] = _heads(sb[:, 2 * d:], h, SB_HEAD_DIM).astype(BF16)


def _l0_proj(x, g, w_in, q_norm, w_uq, kv_norm, w_ukv, cos_t, sin_t):
    b, seq, dm = x.shape
    tm = min(ROW_TILE, seq)
    half = MLA_ROPE // 2
    o_kr = Q_LORA + KV_LORA
    o_sb = o_kr + MLA_ROPE
    zpad = jnp.zeros((dm, LANES - half), F32)
    w_cq = w_in[:, :Q_LORA].astype(BF16)
    w_ckv = w_in[:, Q_LORA:o_kr].astype(BF16)
    w_kr = jnp.concatenate([w_in[:, o_kr:o_kr + half], zpad, w_in[:, o_kr + half:o_sb], zpad],
                           axis=1).astype(BF16)
    w_sb = w_in[:, o_sb:].astype(BF16)
    uq = w_uq.reshape(Q_LORA, MLA_HEADS, MLA_NOPE + MLA_ROPE)
    w_uqn = uq[:, :, :MLA_NOPE].reshape(Q_LORA, -1).astype(BF16)
    w_uqr = jnp.concatenate([uq[:, :, MLA_NOPE:MLA_NOPE + half].reshape(Q_LORA, -1),
                             uq[:, :, MLA_NOPE + half:].reshape(Q_LORA, -1)], axis=1).astype(BF16)
    ukv = w_ukv.reshape(KV_LORA, MLA_HEADS, MLA_NOPE + MLA_V)
    w_ukk = ukv[:, :, :MLA_NOPE].reshape(KV_LORA, -1).astype(BF16)
    w_ukv2 = ukv[:, :, MLA_NOPE:].reshape(KV_LORA, -1).T.astype(BF16)
    tab = pl.BlockSpec((tm, LANES), lambda bi, i: (i, 0))
    heads = lambda n, d: pl.BlockSpec((None, n, tm, d), lambda bi, i: (bi, 0, i, 0))
    out_dims = [(MLA_HEADS, MLA_NOPE + MLA_ROPE), (MLA_HEADS, MLA_NOPE + MLA_ROPE), (MLA_HEADS, MLA_V),
                (SB_HEADS, SB_HEAD_DIM), (SB_HEADS, SB_HEAD_DIM), (SB_HEADS, SB_HEAD_DIM)]
    return pl.pallas_call(
        _l0_proj_kernel,
        grid=(b, seq // tm),
        in_specs=[pl.BlockSpec((None, tm, dm), lambda bi, i: (bi, i, 0)), _full((1, dm)), _full(w_cq.shape),
                  _full(w_ckv.shape), _full(w_kr.shape), _full(w_sb.shape), _full((1, Q_LORA)),
                  _full(w_uqn.shape), _full(w_uqr.shape), _full((1, KV_LORA)), _full(w_ukk.shape),
                  _full(w_ukv2.shape), tab, tab],
        out_specs=[pl.BlockSpec((None, n, d, tm), lambda bi, i: (bi, 0, 0, i)) if idx == 2 else heads(n, d)
                   for idx, (n, d) in enumerate(out_dims)],
        out_shape=[jax.ShapeDtypeStruct((b, n, d, seq) if idx == 2 else (b, n, seq, d), BF16)
                   for idx, (n, d) in enumerate(out_dims)],
        compiler_params=_params("parallel", "parallel"),
    )(x, g.reshape(1, dm), w_cq, w_ckv, w_kr, w_sb, q_norm.reshape(1, -1), w_uqn, w_uqr,
      kv_norm.reshape(1, -1), w_ukk, w_ukv2, cos_t, sin_t)


def _mla_kernel(q_ref, k_ref, vt_ref, o_ref, m_sc, l_sc, acc_sc, *, t, sub, c):
    qi = pl.program_id(2)
    m_sc[...] = jnp.full_like(m_sc, -jnp.inf)
    l_sc[...] = jnp.zeros_like(l_sc)
    acc_sc[...] = jnp.zeros_like(acc_sc)

    def tile(j, diagonal):
        off = pl.multiple_of(j * t, t)
        k = k_ref[pl.ds(off, t), :]
        m_all, l_all, acc_all = m_sc[...], l_sc[...], acc_sc[...]
        m_out, l_out, acc_out = [], [], []
        heights = [c0 + sub if diagonal else t for c0 in range(0, t, sub)]
        logits = [_dot_nt(k[:hgt], q_ref[pl.ds(c0, sub), :]) for c0, hgt in zip(range(0, t, sub), heights)]
        for c0, hgt, s in zip(range(0, t, sub), heights, logits):
            if diagonal:
                key = lax.broadcasted_iota(jnp.int32, (hgt, sub), 0)
                qry = lax.broadcasted_iota(jnp.int32, (hgt, sub), 1) + c0
                s = jnp.where(key <= qry, s, -jnp.inf)
            m_prev = m_all[:, c0:c0 + sub]
            m_new = jnp.maximum(m_prev, jnp.max(s, axis=0, keepdims=True))
            alpha = jnp.exp2((m_prev - m_new) * c)
            p = jnp.exp2((s - m_new) * c)
            l_out.append(alpha * l_all[:, c0:c0 + sub] + jnp.sum(p, axis=0, keepdims=True))
            acc_out.append(alpha * acc_all[:, c0:c0 + sub] + _dot(vt_ref[:, pl.ds(off, hgt)], p.astype(BF16)))
            m_out.append(m_new)
        m_sc[...] = jnp.concatenate(m_out, axis=1)
        l_sc[...] = jnp.concatenate(l_out, axis=1)
        acc_sc[...] = jnp.concatenate(acc_out, axis=1)

    def body(j, carry):
        tile(j, False)
        return carry

    lax.fori_loop(0, qi, body, 0)
    tile(qi, True)
    o_ref[...] = (acc_sc[...] / l_sc[...]).astype(o_ref.dtype)


def _mla_attention(q, k, v_t):
    b, h, s, dq = q.shape
    dv = v_t.shape[2]
    t = min(MLA_TILE, s)
    c = (MLA_NOPE + MLA_ROPE) ** -0.5 * LOG2E
    return pl.pallas_call(
        functools.partial(_mla_kernel, t=t, sub=min(MLA_SUB, t), c=c),
        grid=(b, h, s // t),
        in_specs=[pl.BlockSpec((None, None, t, dq), lambda bi, hi, i: (bi, hi, i, 0)),
                  pl.BlockSpec((None, None, s, dq), lambda bi, hi, i: (bi, hi, 0, 0)),
                  pl.BlockSpec((None, None, dv, s), lambda bi, hi, i: (bi, hi, 0, 0))],
        out_specs=pl.BlockSpec((None, None, dv, t), lambda bi, hi, i: (bi, hi, 0, i)),
        out_shape=jax.ShapeDtypeStruct((b, h, dv, s), BF16),
        scratch_shapes=[pltpu.VMEM((1, t), F32), pltpu.VMEM((1, t), F32), pltpu.VMEM((dv, t), F32)],
        compiler_params=_params("parallel", "parallel", "arbitrary"),
    )(q, k, v_t)


def _split3(x):
    hi = x.astype(BF16)
    r = x - hi.astype(F32)
    mid = r.astype(BF16)
    lo = (r - mid.astype(F32)).astype(BF16)
    return hi, mid, lo


def _sb_kernel(q_ref, k_ref, v_ref, o_ref, acc_sc, run_sc, *, t):
    qi = pl.program_id(2)
    q = q_ref[...]
    acc_sc[...] = jnp.zeros_like(acc_sc)
    run_sc[...] = jnp.zeros_like(run_sc)
    r = lax.broadcasted_iota(jnp.int32, (t, t), 0)
    cc = lax.broadcasted_iota(jnp.int32, (t, t), 1)
    later = (r > cc).astype(BF16)

    past = cc < r

    def scores(j):
        return _dot_nt(q, k_ref[pl.ds(pl.multiple_of(j * t, t), t), :])

    def log_terms(z, diagonal):
        log_fail = jnp.minimum(-z, 0.0) - jnp.log(1.0 + jnp.exp(-jnp.abs(z)))
        log_hit = log_fail + z
        if diagonal:
            log_fail = jnp.where(past, log_fail, 0.0)
        hi, mid, lo = _split3(log_fail)
        within = _dot(hi, later) + _dot(mid, later) + _dot(lo, later)
        return log_hit + within, jnp.sum(log_fail, axis=-1, keepdims=True)

    def weighted(j, log_w, run, diagonal):
        w = jnp.exp(log_w + run)
        if diagonal:
            w = jnp.where(past, w, 0.0)
        return _dot(w.astype(BF16), v_ref[pl.ds(pl.multiple_of(j * t, t), t), :])

    def tile(j, diagonal):
        log_w, total = log_terms(scores(j), diagonal)
        acc_sc[...] += weighted(j, log_w, run_sc[...], diagonal)
        run_sc[...] += total

    @pl.when(qi == 0)
    def _():
        tile(qi, True)

    @pl.when(qi > 0)
    def _():
        z_diag, z_prev = scores(qi), scores(qi - 1)
        lw_diag, total_diag = log_terms(z_diag, True)
        lw_prev, total_prev = log_terms(z_prev, False)
        acc_sc[...] = weighted(qi, lw_diag, 0.0, True) + weighted(qi - 1, lw_prev, total_diag, False)
        run_sc[...] = total_diag + total_prev

    def cond(carry):
        j, worst = carry
        return jnp.logical_and(j >= 0, worst > SB_UNDERFLOW)

    def body(carry):
        j, _ = carry
        tile(j, False)
        return j - 1, jnp.max(run_sc[...])

    lax.while_loop(cond, body, (qi - 2, jnp.max(run_sc[...])))
    o_ref[...] = acc_sc[...].astype(o_ref.dtype)


def _sb_attention(q, k, v):
    b, h, s, d = q.shape
    t = min(SB_TILE, s)
    return pl.pallas_call(
        functools.partial(_sb_kernel, t=t),
        grid=(b, h, s // t),
        in_specs=[pl.BlockSpec((None, None, t, d), lambda bi, hi, i: (bi, hi, i, 0)),
                  pl.BlockSpec((None, None, s, d), lambda bi, hi, i: (bi, hi, 0, 0)),
                  pl.BlockSpec((None, None, s, d), lambda bi, hi, i: (bi, hi, 0, 0))],
        out_specs=pl.BlockSpec((None, None, t, d), lambda bi, hi, i: (bi, hi, i, 0)),
        out_shape=jax.ShapeDtypeStruct((b, h, s, d), BF16),
        scratch_shapes=[pltpu.VMEM((t, d), F32), pltpu.VMEM((t, 1), F32)],
        compiler_params=_params("parallel", "parallel", "arbitrary"),
    )(q, k, v)


def _moba_select_kernel(q_ref, k_ref, qaug_ref, kmean_sc, *, nb, tiles):
    blk = MOBA_BLOCK
    d = MOBA_HEAD_DIM
    step = pl.program_id(2)

    @pl.when(step == 0)
    def _():
        kf = k_ref[:, pl.ds(0, d)].astype(F32).reshape(nb, blk, d)
        kmean_sc[...] = jnp.sum(kf, axis=1) * (1.0 / blk)

    k1, k2, k3 = _split3(kmean_sc[...])
    row = lax.broadcasted_iota(jnp.int32, (nb, blk), 0)
    eye = (lax.broadcasted_iota(jnp.int32, (nb, nb), 0) == lax.broadcasted_iota(jnp.int32, (nb, nb), 1))
    tile_rows = [pl.ds(t * blk, blk) for t in range(tiles)]
    gates = [_dot_nt(k1, q_ref[rows, :]) + _dot_nt(k2, q_ref[rows, :]) + _dot_nt(k3, q_ref[rows, :])
             for rows in tile_rows]
    for t, (rows, gate) in enumerate(zip(tile_rows, gates)):
        i = step * tiles + t
        q = q_ref[rows, :]
        gate = jnp.where(row < i, gate, -jnp.inf)
        masked = jnp.where(row == i, 0.0, 1.0)
        for _ in range(MOBA_TOPK):
            mx = jnp.max(gate, axis=0, keepdims=True)
            is_max = jnp.logical_and(gate == mx, mx > -jnp.inf)
            idx = jnp.min(jnp.where(is_max, row, nb), axis=0, keepdims=True)
            pick = row == idx
            masked = jnp.where(pick, 0.0, masked)
            gate = jnp.where(pick, -jnp.inf, gate)
        masked_t = lax.dot_general(masked.astype(BF16), eye.astype(BF16), (((0,), (0,)), ((), ())),
                                   preferred_element_type=F32)
        qaug_ref[rows, pl.ds(0, d)] = q
        qaug_ref[rows, pl.ds(d, nb)] = masked_t.astype(BF16)


def _moba_select(q, k_aug):
    b, h, s, d = q.shape
    nb = s // MOBA_BLOCK
    tiles = min(MOBA_SELECT_TILES, nb)
    t = tiles * MOBA_BLOCK
    return pl.pallas_call(
        functools.partial(_moba_select_kernel, nb=nb, tiles=tiles),
        grid=(b, h, s // t),
        in_specs=[pl.BlockSpec((None, None, t, d), lambda bi, hi, i: (bi, hi, i, 0)),
                  pl.BlockSpec((None, None, s, d + nb), lambda bi, hi, i: (bi, hi, 0, 0))],
        out_specs=pl.BlockSpec((None, None, t, d + nb), lambda bi, hi, i: (bi, hi, i, 0)),
        out_shape=jax.ShapeDtypeStruct((b, h, s, d + nb), BF16),
        scratch_shapes=[pltpu.VMEM((nb, d), F32)],
        compiler_params=_params("parallel", "parallel", "arbitrary"),
    )(q, k_aug)


def _moba_kernel(qaug_ref, k_ref, vt_ref, bown_ref, bprev_ref, bfar_ref, o_ref,
                 m_sc, l_sc, acc_sc, la_sc, lb_sc, *, far):
    blk = MOBA_BLOCK
    i = pl.program_id(2)
    q = qaug_ref[...]
    n_groups = (jnp.maximum(i, 1) - 1) // far

    def logits(off, n):
        if n < 2:
            return _dot_nt(k_ref[pl.ds(off, n * blk), :], q)
        n0 = n // 2
        return jnp.concatenate([_dot_nt(k_ref[pl.ds(off, n0 * blk), :], q),
                                _dot_nt(k_ref[pl.ds(off + n0 * blk, (n - n0) * blk), :], q)], axis=0)

    tail = i - n_groups * far
    for w in range(far + 1):
        @pl.when(tail == w)
        def _(w=w):
            off = pl.multiple_of((i - w) * blk, blk)
            s = logits(off, w + 1)
            la_sc[...] = logits(0, far)
            key = lax.broadcasted_iota(jnp.int32, (blk, blk), 0)
            qry = lax.broadcasted_iota(jnp.int32, (blk, blk), 1)
            parts = [jnp.where(key <= qry, s[w * blk:] + bown_ref[...], -jnp.inf)]
            if w >= 1:
                parts.insert(0, s[(w - 1) * blk:w * blk] + bprev_ref[...])
            if w >= 2:
                parts.insert(0, s[:(w - 1) * blk] + bfar_ref[...])
            s = parts[0] if len(parts) == 1 else jnp.concatenate(parts, axis=0)
            m0 = jnp.max(s, axis=0, keepdims=True)
            p = jnp.exp(s - m0)
            m_sc[...] = m0
            l_sc[...] = jnp.sum(p, axis=0, keepdims=True)
            acc_sc[...] = _dot(vt_ref[:, pl.ds(off, (w + 1) * blk)], p.astype(BF16))

    def far_logits(g, buf):
        buf[...] = logits(pl.multiple_of(g * (far * blk), far * blk), far)

    def far_consume(g, buf):
        off = pl.multiple_of(g * (far * blk), far * blk)
        s = buf[...]
        shift = bfar_ref[...]
        m_prev = m_sc[...]
        m_new = jnp.maximum(m_prev, jnp.max(s, axis=0, keepdims=True) + shift)
        alpha = jnp.exp(m_prev - m_new)
        ref_point = m_new - shift
        l_new = alpha * l_sc[...]
        acc_new = alpha * acc_sc[...]
        for n in range(far):
            p = jnp.exp(s[n * blk:(n + 1) * blk] - ref_point)
            l_new = l_new + jnp.sum(p, axis=0, keepdims=True)
            acc_new = acc_new + _dot(vt_ref[:, pl.ds(off + n * blk, blk)], p.astype(BF16))
        l_sc[...] = l_new
        acc_sc[...] = acc_new
        m_sc[...] = m_new

    last = n_groups - 1

    def far_pair(pair, carry):
        g = 2 * pair
        far_logits(g + 1, lb_sc)
        far_consume(g, la_sc)
        far_logits(g + 2, la_sc)
        far_consume(g + 1, lb_sc)
        return carry

    lax.fori_loop(0, lax.shift_right_logical(jnp.maximum(last, 0), 1), far_pair, 0)

    @pl.when(jnp.logical_and(n_groups > 0, (last & 1) == 0))
    def _():
        far_consume(last, la_sc)

    @pl.when(jnp.logical_and(n_groups > 0, (last & 1) == 1))
    def _():
        far_logits(last, lb_sc)
        far_consume(last - 1, la_sc)
        far_consume(last, lb_sc)

    o_ref[...] = (acc_sc[...] / l_sc[...]).astype(o_ref.dtype)


def _t5_bucket(dist):
    n = jnp.maximum(dist, 0)
    nf = jnp.maximum(n, 1).astype(F32)
    large = REL_MAX_EXACT + (jnp.log(nf / REL_MAX_EXACT) / math.log(REL_MAX_DIST / REL_MAX_EXACT)
                             * (REL_BUCKETS - REL_MAX_EXACT)).astype(jnp.int32)
    large = jnp.minimum(large, REL_BUCKETS - 1)
    return jnp.where(n < REL_MAX_EXACT, n, large)


def _moba_marks(seq):
    nb = seq // MOBA_BLOCK
    own_block = (jnp.arange(seq)[:, None] // MOBA_BLOCK) == jnp.arange(nb)[None, :]
    return jnp.where(own_block, -MOBA_MASK, 0.0).astype(BF16)


def _moba_attention(q_aug, k_aug, v_t, rel_bias):
    b, h, d, s = v_t.shape
    blk = MOBA_BLOCK
    assert s % blk == 0 and d == MOBA_HEAD_DIM
    assert 2 * blk > REL_MAX_DIST
    nb = s // blk
    far = min(MOBA_FAR, nb)
    offs = jnp.arange(blk)
    dist = offs[None, :] - offs[:, None]

    def bias_table(dist):
        onehot = _t5_bucket(dist)[None, :, :] == jnp.arange(REL_BUCKETS)[:, None, None]
        return jnp.sum(jnp.where(onehot[None], rel_bias[:, :, None, None], 0.0), axis=1)

    b_own = bias_table(dist)
    b_prev = bias_table(dist + blk)
    b_far = jnp.broadcast_to(rel_bias[:, REL_BUCKETS - 1][:, None, None], (h, 1, blk))
    return pl.pallas_call(
        functools.partial(_moba_kernel, far=far),
        grid=(b, h, nb),
        in_specs=[pl.BlockSpec((None, None, blk, d + nb), lambda bi, hi, i: (bi, hi, i, 0)),
                  pl.BlockSpec((None, None, s, d + nb), lambda bi, hi, i: (bi, hi, 0, 0)),
                  pl.BlockSpec((None, None, d, s), lambda bi, hi, i: (bi, hi, 0, 0)),
                  pl.BlockSpec((None, blk, blk), lambda bi, hi, i: (hi, 0, 0)),
                  pl.BlockSpec((None, blk, blk), lambda bi, hi, i: (hi, 0, 0)),
                  pl.BlockSpec((None, 1, blk), lambda bi, hi, i: (hi, 0, 0))],
        out_specs=pl.BlockSpec((None, None, d, blk), lambda bi, hi, i: (bi, hi, 0, i)),
        out_shape=jax.ShapeDtypeStruct((b, h, d, s), BF16),
        scratch_shapes=[pltpu.VMEM((1, blk), F32), pltpu.VMEM((1, blk), F32), pltpu.VMEM((d, blk), F32),
                        pltpu.VMEM((far * blk, blk), F32), pltpu.VMEM((far * blk, blk), F32)],
        compiler_params=_params("parallel", "parallel", "arbitrary"),
    )(q_aug, k_aug, v_t, b_own, b_prev, b_far)


def _l1_proj_kernel(x_ref, g_ref, wqk_ref, wvt_ref, marks_ref, q_ref, k_ref, vt_ref):
    d = MOBA_HEAD_DIM
    width = MOBA_HEADS * d
    hn = _rms(x_ref[...], g_ref[...]).astype(BF16)
    y = _dot(hn, wqk_ref[...])
    v_t = _dot_nt(wvt_ref[...], hn)
    marks = marks_ref[...]
    for h in range(MOBA_HEADS):
        q_ref[h] = (_heads(y[:, :width], h, d) * (d ** -0.5)).astype(BF16)
        k_ref[h] = jnp.concatenate([_heads(y[:, width:], h, d).astype(BF16), marks], axis=-1)
        vt_ref[h] = v_t[h * d:(h + 1) * d].astype(BF16)


def _l1_proj(x, g, w_qkv, marks):
    b, seq, dm = x.shape
    tm = min(ROW_TILE, seq)
    nb = marks.shape[1]
    d = MOBA_HEAD_DIM
    width = MOBA_HEADS * d
    w_qk = w_qkv[:, :2 * width].astype(BF16)
    w_vt = w_qkv[:, 2 * width:].T.astype(BF16)
    heads = lambda n: pl.BlockSpec((None, MOBA_HEADS, tm, n), lambda bi, i: (bi, 0, i, 0))
    return pl.pallas_call(
        _l1_proj_kernel,
        grid=(b, seq // tm),
        in_specs=[pl.BlockSpec((None, tm, dm), lambda bi, i: (bi, i, 0)), _full((1, dm)), _full(w_qk.shape),
                  _full(w_vt.shape), pl.BlockSpec((tm, nb), lambda bi, i: (i, 0))],
        out_specs=[heads(d), heads(d + nb),
                   pl.BlockSpec((None, MOBA_HEADS, d, tm), lambda bi, i: (bi, 0, 0, i))],
        out_shape=[jax.ShapeDtypeStruct((b, MOBA_HEADS, seq, d), BF16),
                   jax.ShapeDtypeStruct((b, MOBA_HEADS, seq, d + nb), BF16),
                   jax.ShapeDtypeStruct((b, MOBA_HEADS, d, seq), BF16)],
        compiler_params=_params("parallel", "parallel"),
    )(x, g.reshape(1, dm), w_qk, w_vt, marks)


def _out_proj_kernel(*refs, transposed):
    *a_refs, w_ref, x_ref, o_ref = refs
    y = x_ref[...]
    row0 = 0
    for a_ref, is_t in zip(a_refs, transposed):
        if is_t:
            heads, d, rows = a_ref.shape
            merged_t = a_ref[...].reshape(heads * d, rows)
            y = y + lax.dot_general(merged_t, w_ref[pl.ds(row0, heads * d), :], (((0,), (0,)), ((), ())),
                                    preferred_element_type=F32)
        else:
            heads, rows, d = a_ref.shape
            merged = jnp.concatenate([a_ref[h] for h in range(heads)], axis=-1)
            y = y + _dot(merged, w_ref[pl.ds(row0, heads * d), :])
        row0 += heads * d
    o_ref[...] = y


def _out_proj(mixed, transposed, w, x):
    b, seq, dm = x.shape
    tm = min(ROW_TILE, seq)
    row = pl.BlockSpec((None, tm, dm), lambda bi, i: (bi, i, 0))
    specs = [pl.BlockSpec((None, a.shape[1], a.shape[2], tm), lambda bi, i: (bi, 0, 0, i)) if is_t else
             pl.BlockSpec((None, a.shape[1], tm, a.shape[3]), lambda bi, i: (bi, 0, i, 0))
             for a, is_t in zip(mixed, transposed)]
    return pl.pallas_call(
        functools.partial(_out_proj_kernel, transposed=tuple(transposed)),
        grid=(b, seq // tm),
        in_specs=specs + [_full(w.shape), row],
        out_specs=row,
        out_shape=jax.ShapeDtypeStruct((b, seq, dm), F32),
        compiler_params=_params("parallel", "parallel"),
    )(*mixed, w.astype(BF16), x)


def _silu(x):
    return x / (1.0 + jnp.exp(-x))


def _swiglu_kernel(x_ref, g_ref, wg_ref, wu_ref, wd_ref, o_ref, *, chunk):
    x = x_ref[...]
    hn = _rms(x, g_ref[...]).astype(BF16)
    o_ref[...] = x
    for c0 in range(0, wg_ref.shape[1], chunk):
        a = _silu(_dot(hn, wg_ref[:, c0:c0 + chunk])) * _dot(hn, wu_ref[:, c0:c0 + chunk])
        o_ref[...] += _dot(a.astype(BF16), wd_ref[c0:c0 + chunk, :])


def _swiglu(x2, g, w_gate, w_up, w_down):
    rows, dm = x2.shape
    tm = min(ROW_TILE, rows)
    return pl.pallas_call(
        functools.partial(_swiglu_kernel, chunk=512),
        grid=(rows // tm,),
        in_specs=[pl.BlockSpec((tm, dm), lambda i: (i, 0)), _full((1, dm)), _full(w_gate.shape),
                  _full(w_up.shape), _full(w_down.shape)],
        out_specs=pl.BlockSpec((tm, dm), lambda i: (i, 0)),
        out_shape=jax.ShapeDtypeStruct((rows, dm), F32),
        compiler_params=_params("parallel"),
    )(x2, g.reshape(1, dm), w_gate.astype(BF16), w_up.astype(BF16), w_down.astype(BF16))


def _moe_kernel(x_ref, g_ref, wr_ref, wg_ref, wu_ref, wd_ref, gf_ref, o_ref, hn_sc, gate_sc, acc_sc):
    e = pl.program_id(1)
    lane = lax.broadcasted_iota(jnp.int32, gate_sc.shape, 1)

    @pl.when(e == 0)
    def _():
        h = _rms(x_ref[...], g_ref[...])
        hn_sc[...] = h.astype(BF16)
        logits = jnp.dot(h, wr_ref[...], preferred_element_type=F32, precision=lax.Precision.HIGHEST)
        logits = jnp.where(lane < N_EXPERTS, logits, -jnp.inf)
        v1 = jnp.max(logits, axis=-1, keepdims=True)
        i1 = jnp.min(jnp.where(logits == v1, lane, LANES), axis=-1, keepdims=True)
        rest = jnp.where(lane == i1, -jnp.inf, logits)
        v2 = jnp.max(rest, axis=-1, keepdims=True)
        i2 = jnp.min(jnp.where(rest == v2, lane, LANES), axis=-1, keepdims=True)
        p2 = jnp.exp(v2 - v1)
        w1 = 1.0 / (1.0 + p2)
        gate_sc[...] = jnp.where(lane == i1, w1, 0.0) + jnp.where(lane == i2, p2 * w1, 0.0)
        acc_sc[...] = jnp.zeros_like(acc_sc)

    gate = jnp.sum(jnp.where(lane == e, gate_sc[...], 0.0), axis=-1, keepdims=True)
    tm = hn_sc.shape[0]
    halves = [pl.ds(r0, tm // 2) for r0 in (0, tm // 2)]
    ups = [(_dot(hn_sc[rows, :], wg_ref[...]), _dot(hn_sc[rows, :], wu_ref[...])) for rows in halves]
    acc = acc_sc[...]
    outs = []
    for rows, (g, u) in zip(halves, ups):
        lo = rows.start
        a = _silu(g) * u * gate[lo:lo + tm // 2]
        outs.append(acc[lo:lo + tm // 2] + _dot(a.astype(BF16), wd_ref[...]))
    acc_sc[...] = jnp.concatenate(outs, axis=0)

    @pl.when(e == pl.num_programs(1) - 1)
    def _():
        o_ref[...] = _rms(x_ref[...] + acc_sc[...], gf_ref[...])


def _moe_final(x2, g, w_router, w_gate, w_up, w_down, g_final):
    rows, dm = x2.shape
    tm = min(MOE_ROW_TILE, rows)
    ne, _, dff = w_gate.shape
    w_r = jnp.zeros((dm, LANES), F32).at[:, :ne].set(w_router)
    return pl.pallas_call(
        _moe_kernel,
        grid=(rows // tm, ne),
        in_specs=[pl.BlockSpec((tm, dm), lambda i, e: (i, 0)),
                  pl.BlockSpec((1, dm), lambda i, e: (0, 0)),
                  pl.BlockSpec((dm, LANES), lambda i, e: (0, 0)),
                  pl.BlockSpec((None, dm, dff), lambda i, e: (e, 0, 0)),
                  pl.BlockSpec((None, dm, dff), lambda i, e: (e, 0, 0)),
                  pl.BlockSpec((None, dff, dm), lambda i, e: (e, 0, 0)),
                  pl.BlockSpec((1, dm), lambda i, e: (0, 0))],
        out_specs=pl.BlockSpec((tm, dm), lambda i, e: (i, 0)),
        out_shape=jax.ShapeDtypeStruct((rows, dm), F32),
        scratch_shapes=[pltpu.VMEM((tm, dm), BF16), pltpu.VMEM((tm, LANES), F32), pltpu.VMEM((tm, dm), F32)],
        compiler_params=_params("parallel", "arbitrary"),
    )(x2, g.reshape(1, dm), w_r, w_gate.astype(BF16), w_up.astype(BF16), w_down.astype(BF16),
      g_final.reshape(1, dm))


def _rope_tables(seq):
    half = MLA_ROPE // 2
    inv = 1.0 / (ROPE_THETA ** (jnp.arange(0, MLA_ROPE, 2, dtype=F32) / MLA_ROPE))
    ang = jnp.arange(seq, dtype=F32)[:, None] * inv[None, :]
    reps = LANES // half
    return jnp.tile(jnp.cos(ang), (1, reps)), jnp.tile(jnp.sin(ang), (1, reps))


def kernel(x, ev_attn_norm, ev_w_in, ev_q_norm, ev_w_uq, ev_kv_norm, ev_w_ukv, ev_w_o, ev_ffn_norm, ev_w_gate, ev_w_up, ev_w_down, od_attn_norm, od_w_qkv, od_w_o, od_ffn_norm, od_w_router, od_w_gate, od_w_up, od_w_down, rel_bias, final_norm):
    b, s, dm = x.shape
    cos_t, sin_t = _rope_tables(s)

    q_a, k_a, v_a_t, q_b, k_b, v_b = _l0_proj(x, ev_attn_norm[0], ev_w_in[0], ev_q_norm[0], ev_w_uq[0],
                                            ev_kv_norm[0], ev_w_ukv[0], cos_t, sin_t)
    o_a_t = _mla_attention(q_a, k_a, v_a_t)
    o_b = _sb_attention(q_b, k_b, v_b)
    x = _out_proj([o_a_t, o_b], [True, False], ev_w_o[0], x)
    x = _swiglu(x.reshape(b * s, dm), ev_ffn_norm[0], ev_w_gate[0], ev_w_up[0], ev_w_down[0]).reshape(b, s, dm)

    q, k_aug, v_t = _l1_proj(x, od_attn_norm[0], od_w_qkv[0], _moba_marks(s))
    o_c_t = _moba_attention(_moba_select(q, k_aug), k_aug, v_t, rel_bias)
    x = _out_proj([o_c_t], [True], od_w_o[0], x)
    out = _moe_final(x.reshape(b * s, dm), od_ffn_norm[0], od_w_router[0], od_w_gate[0], od_w_up[0],
                     od_w_down[0], final_norm)
    return out.reshape(b, s, dm)
```

```python
import functools
import math

import jax
import jax.numpy as jnp
import numpy as np
from jax import lax
from jax.experimental import pallas as pl
from jax.experimental.pallas import tpu as pltpu

MLA_HEADS = 8
MLA_NOPE = 64
MLA_ROPE = 32
MLA_V = 64
Q_LORA = 384
KV_LORA = 256
ROPE_THETA = 10000.0
SB_HEADS = 8
SB_HEAD_DIM = 64
MOBA_HEADS = 16
MOBA_HEAD_DIM = 64
MOBA_BLOCK = 256
MOBA_TOPK = 3
REL_BUCKETS = 32
REL_MAX_EXACT = REL_BUCKETS // 2
REL_MAX_DIST = 128
N_EXPERTS = 8
TOP_K = 2
RMS_EPS = 1e-6

LANES = 128
VMEM_LIMIT = 56 * 1024 * 1024
ROW_TILE = 512
MOE_ROW_TILE = 1024
FFN_CHUNK = 512
MLA_TILE = 2048
MLA_SUB = 256
MOBA_FAR = 8
MOBA_SELECT_TILES = 8
MOBA_MASK = 2.0 ** 100
SB_TILE = 256
SB_UNDERFLOW = -128.0
LOG2E = 1.4426950408889634

BF16 = jnp.bfloat16
F32 = jnp.float32


def _params(*sem):
    return pltpu.CompilerParams(dimension_semantics=sem, vmem_limit_bytes=VMEM_LIMIT)


def _rms(x, g):
    return x * lax.rsqrt(jnp.mean(x * x, axis=-1, keepdims=True) + RMS_EPS) * g


def _dot(a, b):
    return jnp.dot(a, b, preferred_element_type=F32)


def _dot_nt(a, b):
    return lax.dot_general(a, b, (((1,), (1,)), ((), ())), preferred_element_type=F32)


def _full(shape):
    return pl.BlockSpec(shape, lambda *_: (0,) * len(shape))


def _heads(y, h, d):
    return y[:, h * d:(h + 1) * d]


def _l0_proj_kernel(x_ref, g_ref, wcq_ref, wckv_ref, wkr_ref, wsb_ref, qn_ref, wuqn_ref, wuqr_ref,
                    kvn_ref, wukk_ref, wukv_ref, cos_ref, sin_ref,
                    qa_ref, ka_ref, va_ref, sbq_ref, sbk_ref, sbv_ref):
    hn = _rms(x_ref[...], g_ref[...]).astype(BF16)
    c = cos_ref[...]
    s = sin_ref[...]
    half = MLA_ROPE // 2

    def rope(t):
        t1 = t[:, :LANES]
        t2 = t[:, LANES:]
        return t1 * c - t2 * s, t2 * c + t1 * s

    cq = _rms(_dot(hn, wcq_ref[...]), qn_ref[...]).astype(BF16)
    qn = _dot(cq, wuqn_ref[...])
    r1, r2 = rope(_dot(cq, wuqr_ref[...]))
    ckv = _rms(_dot(hn, wckv_ref[...]), kvn_ref[...]).astype(BF16)
    kn = _dot(ckv, wukk_ref[...])
    va_t = _dot_nt(wukv_ref[...], ckv)
    k1, k2 = rope(_dot(hn, wkr_ref[...]))
    k_rope = jnp.concatenate([k1[:, :half], k2[:, :half]], axis=-1).astype(BF16)
    for h in range(MLA_HEADS):
        qa_ref[h] = jnp.concatenate([_heads(qn, h, MLA_NOPE), _heads(r1, h, half), _heads(r2, h, half)],
                                    axis=-1).astype(BF16)
        ka_ref[h] = jnp.concatenate([_heads(kn, h, MLA_NOPE).astype(BF16), k_rope], axis=-1)
        va_ref[h] = va_t[h * MLA_V:(h + 1) * MLA_V].astype(BF16)

    d = SB_HEADS * SB_HEAD_DIM
    sb = _dot(hn, wsb_ref[...])
    for h in range(SB_HEADS):
        sbq_ref[h] = (_heads(sb[:, :d], h, SB_HEAD_DIM) * (SB_HEAD_DIM ** -0.5)).astype(BF16)
        sbk_ref[h] = _heads(sb[:, d:2 * d], h, SB_HEAD_DIM).astype(BF16)
        sbv_ref[h] = _heads(sb[:, 2 * d:], h, SB_HEAD_DIM).astype(BF16)


def _l0_proj(x, g, w_in, q_norm, w_uq, kv_norm, w_ukv, cos_t, sin_t):
    b, seq, dm = x.shape
    tm = min(ROW_TILE, seq)
    half = MLA_ROPE // 2
    o_kr = Q_LORA + KV_LORA
    o_sb = o_kr + MLA_ROPE
    zpad = jnp.zeros((dm, LANES - half), F32)
    w_cq = w_in[:, :Q_LORA].astype(BF16)
    w_ckv = w_in[:, Q_LORA:o_kr].astype(BF16)
    w_kr = jnp.concatenate([w_in[:, o_kr:o_kr + half], zpad, w_in[:, o_kr + half:o_sb], zpad],
                           axis=1).astype(BF16)
    w_sb = w_in[:, o_sb:].astype(BF16)
    uq = w_uq.reshape(Q_LORA, MLA_HEADS, MLA_NOPE + MLA_ROPE)
    w_uqn = uq[:, :, :MLA_NOPE].reshape(Q_LORA, -1).astype(BF16)
    w_uqr = jnp.concatenate([uq[:, :, MLA_NOPE:MLA_NOPE + half].reshape(Q_LORA, -1),
                             uq[:, :, MLA_NOPE + half:].reshape(Q_LORA, -1)], axis=1).astype(BF16)
    ukv = w_ukv.reshape(KV_LORA, MLA_HEADS, MLA_NOPE + MLA_V)
    w_ukk = ukv[:, :, :MLA_NOPE].reshape(KV_LORA, -1).astype(BF16)
    w_ukv2 = ukv[:, :, MLA_NOPE:].reshape(KV_LORA, -1).T.astype(BF16)
    tab = pl.BlockSpec((tm, LANES), lambda bi, i: (i, 0))
    heads = lambda n, d: pl.BlockSpec((None, n, tm, d), lambda bi, i: (bi, 0, i, 0))
    out_dims = [(MLA_HEADS, MLA_NOPE + MLA_ROPE), (MLA_HEADS, MLA_NOPE + MLA_ROPE), (MLA_HEADS, MLA_V),
                (SB_HEADS, SB_HEAD_DIM), (SB_HEADS, SB_HEAD_DIM), (SB_HEADS, SB_HEAD_DIM)]
    return pl.pallas_call(
        _l0_proj_kernel,
        grid=(b, seq // tm),
        in_specs=[pl.BlockSpec((None, tm, dm), lambda bi, i: (bi, i, 0)), _full((1, dm)), _full(w_cq.shape),
                  _full(w_ckv.shape), _full(w_kr.shape), _full(w_sb.shape), _full((1, Q_LORA)),
                  _full(w_uqn.shape), _full(w_uqr.shape), _full((1, KV_LORA)), _full(w_ukk.shape),
                  _full(w_ukv2.shape), tab, tab],
        out_specs=[pl.BlockSpec((None, n, d, tm), lambda bi, i: (bi, 0, 0, i)) if idx == 2 else heads(n, d)
                   for idx, (n, d) in enumerate(out_dims)],
        out_shape=[jax.ShapeDtypeStruct((b, n, d, seq) if idx == 2 else (b, n, seq, d), BF16)
                   for idx, (n, d) in enumerate(out_dims)],
        compiler_params=_params("parallel", "parallel"),
    )(x, g.reshape(1, dm), w_cq, w_ckv, w_kr, w_sb, q_norm.reshape(1, -1), w_uqn, w_uqr,
      kv_norm.reshape(1, -1), w_ukk, w_ukv2, cos_t, sin_t)


def _mla_kernel(q_ref, k_ref, vt_ref, o_ref, m_sc, l_sc, acc_sc, *, t, sub, c):
    qi = pl.program_id(2)
    m_sc[...] = jnp.full_like(m_sc, -jnp.inf)
    l_sc[...] = jnp.zeros_like(l_sc)
    acc_sc[...] = jnp.zeros_like(acc_sc)

    def tile(j, diagonal):
        off = pl.multiple_of(j * t, t)
        k = k_ref[pl.ds(off, t), :]
        m_all, l_all, acc_all = m_sc[...], l_sc[...], acc_sc[...]
        m_out, l_out, acc_out = [], [], []
        heights = [c0 + sub if diagonal else t for c0 in range(0, t, sub)]
        logits = [_dot_nt(k[:hgt], q_ref[pl.ds(c0, sub), :]) for c0, hgt in zip(range(0, t, sub), heights)]
        for c0, hgt, s in zip(range(0, t, sub), heights, logits):
            if diagonal:
                key = lax.broadcasted_iota(jnp.int32, (hgt, sub), 0)
                qry = lax.broadcasted_iota(jnp.int32, (hgt, sub), 1) + c0
                s = jnp.where(key <= qry, s, -jnp.inf)
            m_prev = m_all[:, c0:c0 + sub]
            m_new = jnp.maximum(m_prev, jnp.max(s, axis=0, keepdims=True))
            alpha = jnp.exp2((m_prev - m_new) * c)
            p = jnp.exp2((s - m_new) * c)
            l_out.append(alpha * l_all[:, c0:c0 + sub] + jnp.sum(p, axis=0, keepdims=True))
            acc_out.append(alpha * acc_all[:, c0:c0 + sub] + _dot(vt_ref[:, pl.ds(off, hgt)], p.astype(BF16)))
            m_out.append(m_new)
        m_sc[...] = jnp.concatenate(m_out, axis=1)
        l_sc[...] = jnp.concatenate(l_out, axis=1)
        acc_sc[...] = jnp.concatenate(acc_out, axis=1)

    def body(j, carry):
        tile(j, False)
        return carry

    lax.fori_loop(0, qi, body, 0)
    tile(qi, True)
    o_ref[...] = (acc_sc[...] / l_sc[...]).astype(o_ref.dtype)


def _mla_attention(q, k, v_t):
    b, h, s, dq = q.shape
    dv = v_t.shape[2]
    t = min(MLA_TILE, s)
    c = (MLA_NOPE + MLA_ROPE) ** -0.5 * LOG2E
    return pl.pallas_call(
        functools.partial(_mla_kernel, t=t, sub=min(MLA_SUB, t), c=c),
        grid=(b, h, s // t),
        in_specs=[pl.BlockSpec((None, None, t, dq), lambda bi, hi, i: (bi, hi, i, 0)),
                  pl.BlockSpec((None, None, s, dq), lambda bi, hi, i: (bi, hi, 0, 0)),
                  pl.BlockSpec((None, None, dv, s), lambda bi, hi, i: (bi, hi, 0, 0))],
        out_specs=pl.BlockSpec((None, None, dv, t), lambda bi, hi, i: (bi, hi, 0, i)),
        out_shape=jax.ShapeDtypeStruct((b, h, dv, s), BF16),
        scratch_shapes=[pltpu.VMEM((1, t), F32), pltpu.VMEM((1, t), F32), pltpu.VMEM((dv, t), F32)],
        compiler_params=_params("parallel", "parallel", "arbitrary"),
    )(q, k, v_t)


def _split3(x):
    hi = x.astype(BF16)
    r = x - hi.astype(F32)
    mid = r.astype(BF16)
    lo = (r - mid.astype(F32)).astype(BF16)
    return hi, mid, lo


def _sb_kernel(q_ref, k_ref, v_ref, o_ref, acc_sc, run_sc, *, t):
    qi = pl.program_id(2)
    q = q_ref[...]
    acc_sc[...] = jnp.zeros_like(acc_sc)
    run_sc[...] = jnp.zeros_like(run_sc)
    r = lax.broadcasted_iota(jnp.int32, (t, t), 0)
    cc = lax.broadcasted_iota(jnp.int32, (t, t), 1)
    later = (r > cc).astype(BF16)

    past = cc < r

    def scores(j):
        return _dot_nt(q, k_ref[pl.ds(pl.multiple_of(j * t, t), t), :])

    def log_terms(z, diagonal):
        log_fail = jnp.minimum(-z, 0.0) - jnp.log(1.0 + jnp.exp(-jnp.abs(z)))
        log_hit = log_fail + z
        if diagonal:
            log_fail = jnp.where(past, log_fail, 0.0)
        hi, mid, lo = _split3(log_fail)
        within = _dot(hi, later) + _dot(mid, later) + _dot(lo, later)
        return log_hit + within, jnp.sum(log_fail, axis=-1, keepdims=True)

    def weighted(j, log_w, run, diagonal):
        w = jnp.exp(log_w + run)
        if diagonal:
            w = jnp.where(past, w, 0.0)
        return _dot(w.astype(BF16), v_ref[pl.ds(pl.multiple_of(j * t, t), t), :])

    def tile(j, diagonal):
        log_w, total = log_terms(scores(j), diagonal)
        acc_sc[...] += weighted(j, log_w, run_sc[...], diagonal)
        run_sc[...] += total

    @pl.when(qi == 0)
    def _():
        tile(qi, True)

    @pl.when(qi > 0)
    def _():
        z_diag, z_prev = scores(qi), scores(qi - 1)
        lw_diag, total_diag = log_terms(z_diag, True)
        lw_prev, total_prev = log_terms(z_prev, False)
        acc_sc[...] = weighted(qi, lw_diag, 0.0, True) + weighted(qi - 1, lw_prev, total_diag, False)
        run_sc[...] = total_diag + total_prev

    def cond(carry):
        j, worst = carry
        return jnp.logical_and(j >= 0, worst > SB_UNDERFLOW)

    def body(carry):
        j, _ = carry
        tile(j, False)
        return j - 1, jnp.max(run_sc[...])

    lax.while_loop(cond, body, (qi - 2, jnp.max(run_sc[...])))
    o_ref[...] = acc_sc[...].astype(o_ref.dtype)


def _sb_attention(q, k, v):
    b, h, s, d = q.shape
    t = min(SB_TILE, s)
    return pl.pallas_call(
        functools.partial(_sb_kernel, t=t),
        grid=(b, h, s // t),
        in_specs=[pl.BlockSpec((None, None, t, d), lambda bi, hi, i: (bi, hi, i, 0)),
                  pl.BlockSpec((None, None, s, d), lambda bi, hi, i: (bi, hi, 0, 0)),
                  pl.BlockSpec((None, None, s, d), lambda bi, hi, i: (bi, hi, 0, 0))],
        out_specs=pl.BlockSpec((None, None, t, d), lambda bi, hi, i: (bi, hi, i, 0)),
        out_shape=jax.ShapeDtypeStruct((b, h, s, d), BF16),
        scratch_shapes=[pltpu.VMEM((t, d), F32), pltpu.VMEM((t, 1), F32)],
        compiler_params=_params("parallel", "parallel", "arbitrary"),
    )(q, k, v)


def _moba_select_kernel(q_ref, k_ref, qaug_ref, kmean_sc, *, nb, tiles):
    blk = MOBA_BLOCK
    d = MOBA_HEAD_DIM
    step = pl.program_id(2)

    @pl.when(step == 0)
    def _():
        kf = k_ref[:, pl.ds(0, d)].astype(F32).reshape(nb, blk, d)
        kmean_sc[...] = jnp.sum(kf, axis=1) * (1.0 / blk)

    k1, k2, k3 = _split3(kmean_sc[...])
    row = lax.broadcasted_iota(jnp.int32, (nb, blk), 0)
    eye = (lax.broadcasted_iota(jnp.int32, (nb, nb), 0) == lax.broadcasted_iota(jnp.int32, (nb, nb), 1))
    tile_rows = [pl.ds(t * blk, blk) for t in range(tiles)]
    gates = [_dot_nt(k1, q_ref[rows, :]) + _dot_nt(k2, q_ref[rows, :]) + _dot_nt(k3, q_ref[rows, :])
             for rows in tile_rows]
    for t, (rows, gate) in enumerate(zip(tile_rows, gates)):
        i = step * tiles + t
        q = q_ref[rows, :]
        gate = jnp.where(row < i, gate, -jnp.inf)
        masked = jnp.where(row == i, 0.0, 1.0)
        for _ in range(MOBA_TOPK):
            mx = jnp.max(gate, axis=0, keepdims=True)
            is_max = jnp.logical_and(gate == mx, mx > -jnp.inf)
            idx = jnp.min(jnp.where(is_max, row, nb), axis=0, keepdims=True)
            pick = row == idx
            masked = jnp.where(pick, 0.0, masked)
            gate = jnp.where(pick, -jnp.inf, gate)
        masked_t = lax.dot_general(masked.astype(BF16), eye.astype(BF16), (((0,), (0,)), ((), ())),
                                   preferred_element_type=F32)
        qaug_ref[rows, pl.ds(0, d)] = q
        qaug_ref[rows, pl.ds(d, nb)] = masked_t.astype(BF16)


def _moba_select(q, k_aug):
    b, h, s, d = q.shape
    nb = s // MOBA_BLOCK
    tiles = min(MOBA_SELECT_TILES, nb)
    t = tiles * MOBA_BLOCK
    return pl.pallas_call(
        functools.partial(_moba_select_kernel, nb=nb, tiles=tiles),
        grid=(b, h, s // t),
        in_specs=[pl.BlockSpec((None, None, t, d), lambda bi, hi, i: (bi, hi, i, 0)),
                  pl.BlockSpec((None, None, s, d + nb), lambda bi, hi, i: (bi, hi, 0, 0))],
        out_specs=pl.BlockSpec((None, None, t, d + nb), lambda bi, hi, i: (bi, hi, i, 0)),
        out_shape=jax.ShapeDtypeStruct((b, h, s, d + nb), BF16),
        scratch_shapes=[pltpu.VMEM((nb, d), F32)],
        compiler_params=_params("parallel", "parallel", "arbitrary"),
    )(q, k_aug)


def _moba_kernel(qaug_ref, k_ref, vt_ref, bown_ref, bprev_ref, bfar_ref, o_ref,
                 m_sc, l_sc, acc_sc, la_sc, lb_sc, *, far):
    blk = MOBA_BLOCK
    i = pl.program_id(2)
    q = qaug_ref[...]
    n_groups = (jnp.maximum(i, 1) - 1) // far

    def logits(off, n):
        if n < 2:
            return _dot_nt(k_ref[pl.ds(off, n * blk), :], q)
        n0 = n // 2
        return jnp.concatenate([_dot_nt(k_ref[pl.ds(off, n0 * blk), :], q),
                                _dot_nt(k_ref[pl.ds(off + n0 * blk, (n - n0) * blk), :], q)], axis=0)

    tail = i - n_groups * far
    for w in range(far + 1):
        @pl.when(tail == w)
        def _(w=w):
            off = pl.multiple_of((i - w) * blk, blk)
            s = logits(off, w + 1)
            la_sc[...] = logits(0, far)
            key = lax.broadcasted_iota(jnp.int32, (blk, blk), 0)
            qry = lax.broadcasted_iota(jnp.int32, (blk, blk), 1)
            parts = [jnp.where(key <= qry, s[w * blk:] + bown_ref[...], -jnp.inf)]
            if w >= 1:
                parts.insert(0, s[(w - 1) * blk:w * blk] + bprev_ref[...])
            if w >= 2:
                parts.insert(0, s[:(w - 1) * blk] + bfar_ref[...])
            s = parts[0] if len(parts) == 1 else jnp.concatenate(parts, axis=0)
            m0 = jnp.max(s, axis=0, keepdims=True)
            p = jnp.exp(s - m0)
            m_sc[...] = m0
            l_sc[...] = jnp.sum(p, axis=0, keepdims=True)
            acc_sc[...] = _dot(vt_ref[:, pl.ds(off, (w + 1) * blk)], p.astype(BF16))

    def far_logits(g, buf):
        buf[...] = logits(pl.multiple_of(g * (far * blk), far * blk), far)

    def far_consume(g, buf):
        off = pl.multiple_of(g * (far * blk), far * blk)
        s = buf[...]
        shift = bfar_ref[...]
        m_prev = m_sc[...]
        m_new = jnp.maximum(m_prev, jnp.max(s, axis=0, keepdims=True) + shift)
        alpha = jnp.exp(m_prev - m_new)
        ref_point = m_new - shift
        l_new = alpha * l_sc[...]
        acc_new = alpha * acc_sc[...]
        for n in range(far):
            p = jnp.exp(s[n * blk:(n + 1) * blk] - ref_point)
            l_new = l_new + jnp.sum(p, axis=0, keepdims=True)
            acc_new = acc_new + _dot(vt_ref[:, pl.ds(off + n * blk, blk)], p.astype(BF16))
        l_sc[...] = l_new
        acc_sc[...] = acc_new
        m_sc[...] = m_new

    last = n_groups - 1

    def far_pair(pair, carry):
        g = 2 * pair
        far_logits(g + 1, lb_sc)
        far_consume(g, la_sc)
        far_logits(g + 2, la_sc)
        far_consume(g + 1, lb_sc)
        return carry

    lax.fori_loop(0, lax.shift_right_logical(jnp.maximum(last, 0), 1), far_pair, 0)

    @pl.when(jnp.logical_and(n_groups > 0, (last & 1) == 0))
    def _():
        far_consume(last, la_sc)

    @pl.when(jnp.logical_and(n_groups > 0, (last & 1) == 1))
    def _():
        far_logits(last, lb_sc)
        far_consume(last - 1, la_sc)
        far_consume(last, lb_sc)

    o_ref[...] = (acc_sc[...] / l_sc[...]).astype(o_ref.dtype)


def _t5_bucket(dist):
    n = jnp.maximum(dist, 0)
    nf = jnp.maximum(n, 1).astype(F32)
    large = REL_MAX_EXACT + (jnp.log(nf / REL_MAX_EXACT) / math.log(REL_MAX_DIST / REL_MAX_EXACT)
                             * (REL_BUCKETS - REL_MAX_EXACT)).astype(jnp.int32)
    large = jnp.minimum(large, REL_BUCKETS - 1)
    return jnp.where(n < REL_MAX_EXACT, n, large)


def _moba_marks(seq):
    nb = seq // MOBA_BLOCK
    own_block = (jnp.arange(seq)[:, None] // MOBA_BLOCK) == jnp.arange(nb)[None, :]
    return jnp.where(own_block, -MOBA_MASK, 0.0).astype(BF16)


def _moba_attention(q_aug, k_aug, v_t, rel_bias):
    b, h, d, s = v_t.shape
    blk = MOBA_BLOCK
    assert s % blk == 0 and d == MOBA_HEAD_DIM
    assert 2 * blk > REL_MAX_DIST
    nb = s // blk
    far = min(MOBA_FAR, nb)
    offs = jnp.arange(blk)
    dist = offs[None, :] - offs[:, None]

    def bias_table(dist):
        onehot = _t5_bucket(dist)[None, :, :] == jnp.arange(REL_BUCKETS)[:, None, None]
        return jnp.sum(jnp.where(onehot[None], rel_bias[:, :, None, None], 0.0), axis=1)

    b_own = bias_table(dist)
    b_prev = bias_table(dist + blk)
    b_far = jnp.broadcast_to(rel_bias[:, REL_BUCKETS - 1][:, None, None], (h, 1, blk))
    return pl.pallas_call(
        functools.partial(_moba_kernel, far=far),
        grid=(b, h, nb),
        in_specs=[pl.BlockSpec((None, None, blk, d + nb), lambda bi, hi, i: (bi, hi, i, 0)),
                  pl.BlockSpec((None, None, s, d + nb), lambda bi, hi, i: (bi, hi, 0, 0)),
                  pl.BlockSpec((None, None, d, s), lambda bi, hi, i: (bi, hi, 0, 0)),
                  pl.BlockSpec((None, blk, blk), lambda bi, hi, i: (hi, 0, 0)),
                  pl.BlockSpec((None, blk, blk), lambda bi, hi, i: (hi, 0, 0)),
                  pl.BlockSpec((None, 1, blk), lambda bi, hi, i: (hi, 0, 0))],
        out_specs=pl.BlockSpec((None, None, d, blk), lambda bi, hi, i: (bi, hi, 0, i)),
        out_shape=jax.ShapeDtypeStruct((b, h, d, s), BF16),
        scratch_shapes=[pltpu.VMEM((1, blk), F32), pltpu.VMEM((1, blk), F32), pltpu.VMEM((d, blk), F32),
                        pltpu.VMEM((far * blk, blk), F32), pltpu.VMEM((far * blk, blk), F32)],
        compiler_params=_params("parallel", "parallel", "arbitrary"),
    )(q_aug, k_aug, v_t, b_own, b_prev, b_far)


def _l1_proj_kernel(x_ref, g_ref, wqk_ref, wvt_ref, marks_ref, q_ref, k_ref, vt_ref):
    d = MOBA_HEAD_DIM
    width = MOBA_HEADS * d
    hn = _rms(x_ref[...], g_ref[...]).astype(BF16)
    y = _dot(hn, wqk_ref[...])
    v_t = _dot_nt(wvt_ref[...], hn)
    marks = marks_ref[...]
    for h in range(MOBA_HEADS):
        q_ref[h] = (_heads(y[:, :width], h, d) * (d ** -0.5)).astype(BF16)
        k_ref[h] = jnp.concatenate([_heads(y[:, width:], h, d).astype(BF16), marks], axis=-1)
        vt_ref[h] = v_t[h * d:(h + 1) * d].astype(BF16)


def _l1_proj(x, g, w_qkv, marks):
    b, seq, dm = x.shape
    tm = min(ROW_TILE, seq)
    nb = marks.shape[1]
    d = MOBA_HEAD_DIM
    width = MOBA_HEADS * d
    w_qk = w_qkv[:, :2 * width].astype(BF16)
    w_vt = w_qkv[:, 2 * width:].T.astype(BF16)
    heads = lambda n: pl.BlockSpec((None, MOBA_HEADS, tm, n), lambda bi, i: (bi, 0, i, 0))
    return pl.pallas_call(
        _l1_proj_kernel,
        grid=(b, seq // tm),
        in_specs=[pl.BlockSpec((None, tm, dm), lambda bi, i: (bi, i, 0)), _full((1, dm)), _full(w_qk.shape),
                  _full(w_vt.shape), pl.BlockSpec((tm, nb), lambda bi, i: (i, 0))],
        out_specs=[heads(d), heads(d + nb),
                   pl.BlockSpec((None, MOBA_HEADS, d, tm), lambda bi, i: (bi, 0, 0, i))],
        out_shape=[jax.ShapeDtypeStruct((b, MOBA_HEADS, seq, d), BF16),
                   jax.ShapeDtypeStruct((b, MOBA_HEADS, seq, d + nb), BF16),
                   jax.ShapeDtypeStruct((b, MOBA_HEADS, d, seq), BF16)],
        compiler_params=_params("parallel", "parallel"),
    )(x, g.reshape(1, dm), w_qk, w_vt, marks)


def _out_proj_kernel(*refs, transposed, ffn):
    ffn_refs = ()
    if ffn:
        *refs, g_ref, wg_ref, wu_ref, wd_ref, o_ref = refs
        ffn_refs = (g_ref, wg_ref, wu_ref, wd_ref)
        refs = (*refs, o_ref)
    *a_refs, w_ref, x_ref, o_ref = refs
    y = x_ref[...]
    row0 = 0
    for a_ref, is_t in zip(a_refs, transposed):
        if is_t:
            heads, d, rows = a_ref.shape
            merged_t = a_ref[...].reshape(heads * d, rows)
            y = y + lax.dot_general(merged_t, w_ref[pl.ds(row0, heads * d), :], (((0,), (0,)), ((), ())),
                                    preferred_element_type=F32)
        else:
            heads, rows, d = a_ref.shape
            merged = jnp.concatenate([a_ref[h] for h in range(heads)], axis=-1)
            y = y + _dot(merged, w_ref[pl.ds(row0, heads * d), :])
        row0 += heads * d
    o_ref[...] = y
    if ffn:
        g_ref, wg_ref, wu_ref, wd_ref = ffn_refs
        hn = _rms(y, g_ref[...]).astype(BF16)
        for c0 in range(0, wg_ref.shape[1], FFN_CHUNK):
            a = _silu(_dot(hn, wg_ref[:, c0:c0 + FFN_CHUNK])) * _dot(hn, wu_ref[:, c0:c0 + FFN_CHUNK])
            o_ref[...] += _dot(a.astype(BF16), wd_ref[c0:c0 + FFN_CHUNK, :])


def _out_proj(mixed, transposed, w, x, ffn=None):
    b, seq, dm = x.shape
    tm = min(ROW_TILE, seq)
    row = pl.BlockSpec((None, tm, dm), lambda bi, i: (bi, i, 0))
    specs = [pl.BlockSpec((None, a.shape[1], a.shape[2], tm), lambda bi, i: (bi, 0, 0, i)) if is_t else
             pl.BlockSpec((None, a.shape[1], tm, a.shape[3]), lambda bi, i: (bi, 0, i, 0))
             for a, is_t in zip(mixed, transposed)]
    extra = []
    if ffn is not None:
        g, w_gate, w_up, w_down = ffn
        extra = [g.reshape(1, dm), w_gate.astype(BF16), w_up.astype(BF16), w_down.astype(BF16)]
    return pl.pallas_call(
        functools.partial(_out_proj_kernel, transposed=tuple(transposed), ffn=ffn is not None),
        grid=(b, seq // tm),
        in_specs=specs + [_full(w.shape), row] + [_full(e.shape) for e in extra],
        out_specs=row,
        out_shape=jax.ShapeDtypeStruct((b, seq, dm), F32),
        compiler_params=_params("parallel", "parallel"),
    )(*mixed, w.astype(BF16), x, *extra)


def _silu(x):
    return x / (1.0 + jnp.exp(-x))


def _moe_kernel(x_ref, g_ref, wr_ref, wg_ref, wu_ref, wd_ref, gf_ref, o_ref, hn_sc, gate_sc, acc_sc):
    e = pl.program_id(1)
    lane = lax.broadcasted_iota(jnp.int32, gate_sc.shape, 1)

    @pl.when(e == 0)
    def _():
        h = _rms(x_ref[...], g_ref[...])
        hn_sc[...] = h.astype(BF16)
        logits = jnp.dot(h, wr_ref[...], preferred_element_type=F32, precision=lax.Precision.HIGHEST)
        logits = jnp.where(lane < N_EXPERTS, logits, -jnp.inf)
        v1 = jnp.max(logits, axis=-1, keepdims=True)
        i1 = jnp.min(jnp.where(logits == v1, lane, LANES), axis=-1, keepdims=True)
        rest = jnp.where(lane == i1, -jnp.inf, logits)
        v2 = jnp.max(rest, axis=-1, keepdims=True)
        i2 = jnp.min(jnp.where(rest == v2, lane, LANES), axis=-1, keepdims=True)
        p2 = jnp.exp(v2 - v1)
        w1 = 1.0 / (1.0 + p2)
        gate_sc[...] = jnp.where(lane == i1, w1, 0.0) + jnp.where(lane == i2, p2 * w1, 0.0)
        acc_sc[...] = jnp.zeros_like(acc_sc)

    gate = jnp.sum(jnp.where(lane == e, gate_sc[...], 0.0), axis=-1, keepdims=True)
    tm = hn_sc.shape[0]
    halves = [pl.ds(r0, tm // 2) for r0 in (0, tm // 2)]
    ups = [(_dot(hn_sc[rows, :], wg_ref[...]), _dot(hn_sc[rows, :], wu_ref[...])) for rows in halves]
    acc = acc_sc[...]
    outs = []
    for rows, (g, u) in zip(halves, ups):
        lo = rows.start
        a = _silu(g) * u * gate[lo:lo + tm // 2]
        outs.append(acc[lo:lo + tm // 2] + _dot(a.astype(BF16), wd_ref[...]))
    acc_sc[...] = jnp.concatenate(outs, axis=0)

    @pl.when(e == pl.num_programs(1) - 1)
    def _():
        o_ref[...] = _rms(x_ref[...] + acc_sc[...], gf_ref[...])


def _moe_final(x2, g, w_router, w_gate, w_up, w_down, g_final):
    rows, dm = x2.shape
    tm = min(MOE_ROW_TILE, rows)
    ne, _, dff = w_gate.shape
    w_r = jnp.zeros((dm, LANES), F32).at[:, :ne].set(w_router)
    return pl.pallas_call(
        _moe_kernel,
        grid=(rows // tm, ne),
        in_specs=[pl.BlockSpec((tm, dm), lambda i, e: (i, 0)),
                  pl.BlockSpec((1, dm), lambda i, e: (0, 0)),
                  pl.BlockSpec((dm, LANES), lambda i, e: (0, 0)),
                  pl.BlockSpec((None, dm, dff), lambda i, e: (e, 0, 0)),
                  pl.BlockSpec((None, dm, dff), lambda i, e: (e, 0, 0)),
                  pl.BlockSpec((None, dff, dm), lambda i, e: (e, 0, 0)),
                  pl.BlockSpec((1, dm), lambda i, e: (0, 0))],
        out_specs=pl.BlockSpec((tm, dm), lambda i, e: (i, 0)),
        out_shape=jax.ShapeDtypeStruct((rows, dm), F32),
        scratch_shapes=[pltpu.VMEM((tm, dm), BF16), pltpu.VMEM((tm, LANES), F32), pltpu.VMEM((tm, dm), F32)],
        compiler_params=_params("parallel", "arbitrary"),
    )(x2, g.reshape(1, dm), w_r, w_gate.astype(BF16), w_up.astype(BF16), w_down.astype(BF16),
      g_final.reshape(1, dm))


def _rope_tables(seq):
    half = MLA_ROPE // 2
    inv = 1.0 / (ROPE_THETA ** (jnp.arange(0, MLA_ROPE, 2, dtype=F32) / MLA_ROPE))
    ang = jnp.arange(seq, dtype=F32)[:, None] * inv[None, :]
    reps = LANES // half
    return jnp.tile(jnp.cos(ang), (1, reps)), jnp.tile(jnp.sin(ang), (1, reps))


def kernel(x, ev_attn_norm, ev_w_in, ev_q_norm, ev_w_uq, ev_kv_norm, ev_w_ukv, ev_w_o, ev_ffn_norm, ev_w_gate, ev_w_up, ev_w_down, od_attn_norm, od_w_qkv, od_w_o, od_ffn_norm, od_w_router, od_w_gate, od_w_up, od_w_down, rel_bias, final_norm):
    b, s, dm = x.shape
    cos_t, sin_t = _rope_tables(s)

    q_a, k_a, v_a_t, q_b, k_b, v_b = _l0_proj(x, ev_attn_norm[0], ev_w_in[0], ev_q_norm[0], ev_w_uq[0],
                                            ev_kv_norm[0], ev_w_ukv[0], cos_t, sin_t)
    o_a_t = _mla_attention(q_a, k_a, v_a_t)
    o_b = _sb_attention(q_b, k_b, v_b)
    x = _out_proj([o_a_t, o_b], [True, False], ev_w_o[0], x,
                  ffn=(ev_ffn_norm[0], ev_w_gate[0], ev_w_up[0], ev_w_down[0]))

    q, k_aug, v_t = _l1_proj(x, od_attn_norm[0], od_w_qkv[0], _moba_marks(s))
    o_c_t = _moba_attention(_moba_select(q, k_aug), k_aug, v_t, rel_bias)
    x = _out_proj([o_c_t], [True], od_w_o[0], x)
    out = _moe_final(x.reshape(b * s, dm), od_ffn_norm[0], od_w_router[0], od_w_gate[0], od_w_up[0],
                     od_w_down[0], final_norm)
    return out.reshape(b, s, dm)
```
